```python
import math
import jax, jax.numpy as jnp
from jax import lax
import numpy as np

D_MODEL = 1024
BATCH = 8
SEQ = 4096
DEPTH = 4

CHUNK = 64
EPS = 1e-6

SSM_HEADS = 16
SSM_HEAD_DIM = 64
SSM_D_INNER = SSM_HEADS * SSM_HEAD_DIM
SSM_GROUPS = 2
SSM_STATE = 128
SSM_CONV = 4
SSM_XBC = SSM_D_INNER + 2 * SSM_GROUPS * SSM_STATE

GDN_HEADS = 8
GDN_DK = 128
GDN_DV = 128
GDN_KEY = GDN_HEADS * GDN_DK
GDN_VAL = GDN_HEADS * GDN_DV
GDN_CONV = 4
GDN_QKV = 2 * GDN_KEY + GDN_VAL

D_FF = 4 * D_MODEL
FFN_CONV = 3

PLE_DIM = 256

IN_SIZES = (SSM_D_INNER, SSM_XBC, SSM_HEADS, GDN_QKV, GDN_VAL, GDN_HEADS, GDN_HEADS, 2 * D_MODEL)
IN_SPLITS = tuple(sum(IN_SIZES[:j + 1]) for j in range(len(IN_SIZES) - 1))
D_IN = sum(IN_SIZES)

kernel_name = 'hybrid_ssd_deltanet_parallel_gated_block'


def rms_norm(x, g):
    xf = x.astype(jnp.float32)
    y = xf * lax.rsqrt(jnp.mean(xf * xf, axis=-1, keepdims=True) + EPS)
    return (y * g.astype(jnp.float32)).astype(x.dtype)


def causal_dwconv(x, w):
    k, c = w.shape
    return lax.conv_general_dilated(
        x, w[:, None, :].astype(x.dtype), window_strides=(1,), padding=[(k - 1, 0)],
        dimension_numbers=('NWC', 'WIO', 'NWC'), feature_group_count=c)


def l2_normalize(x):
    return x * lax.rsqrt(jnp.sum(x * x, axis=-1, keepdims=True) + EPS)


def ssd_chunked(xh, dt, a, bm, cm):
    bsz, s, h, pd = xh.shape
    g, n = bm.shape[2], bm.shape[3]
    r = h // g
    c = s // CHUNK
    xc = (xh * dt[..., None]).reshape(bsz, c, CHUNK, g, r, pd)
    a_dt = jnp.moveaxis((dt * a).reshape(bsz, c, CHUNK, g, r), 2, -1)
    a_cum = jnp.cumsum(a_dt, axis=-1)
    bc = bm.reshape(bsz, c, CHUNK, g, n)
    cc = cm.reshape(bsz, c, CHUNK, g, n)
    causal = jnp.tril(jnp.ones((CHUNK, CHUNK), dtype=bool))
    seg = a_cum[..., :, None] - a_cum[..., None, :]
    lmat = jnp.exp(jnp.where(causal, seg, -jnp.inf))
    scores = jnp.einsum('bclgn,bcsgn->bcgls', cc, bc)
    y_diag = jnp.einsum('bcgrls,bcsgrp->bclgrp', scores[:, :, :, None] * lmat, xc)
    decay_states = jnp.exp(a_cum[..., -1:] - a_cum)
    states = jnp.einsum('bclgn,bcgrl,bclgrp->bcgrpn', bc, decay_states, xc)
    chunk_decay = jnp.exp(a_cum[..., -1])

    def step(carry, inp):
        st, dec = inp
        return carry * dec[..., None, None] + st, carry

    init = jnp.zeros((bsz, g, r, pd, n), jnp.float32)
    _, prev = lax.scan(step, init, (jnp.moveaxis(states, 1, 0), jnp.moveaxis(chunk_decay, 1, 0)))
    prev = jnp.moveaxis(prev, 0, 1)
    y_off = jnp.einsum('bclgn,bcgrpn,bcgrl->bclgrp', cc, prev, jnp.exp(a_cum))
    return (y_diag + y_off).reshape(bsz, s, h, pd)


def mamba2_branch(z, xbc, dt_raw, conv_w, conv_b, dt_bias, a_log, d_skip, norm_g):
    dtype = z.dtype
    xbc = jax.nn.silu(causal_dwconv(xbc, conv_w) + conv_b)
    xs, bm, cm = jnp.split(xbc, [SSM_D_INNER, SSM_D_INNER + SSM_GROUPS * SSM_STATE], axis=-1)
    bsz, s, _ = xs.shape
    f32 = jnp.float32
    xh = xs.reshape(bsz, s, SSM_HEADS, SSM_HEAD_DIM).astype(f32)
    dt = jax.nn.softplus(dt_raw.astype(f32) + dt_bias.astype(f32))
    a = -jnp.exp(a_log.astype(f32))
    y = ssd_chunked(xh, dt, a,
                    bm.reshape(bsz, s, SSM_GROUPS, SSM_STATE).astype(f32),
                    cm.reshape(bsz, s, SSM_GROUPS, SSM_STATE).astype(f32))
    y = y + d_skip.astype(f32)[:, None] * xh
    y = y.reshape(bsz, s, SSM_D_INNER) * jax.nn.silu(z.astype(f32))
    yg = y.reshape(bsz, s, SSM_GROUPS, SSM_D_INNER // SSM_GROUPS)
    yg = yg * lax.rsqrt(jnp.mean(yg * yg, axis=-1, keepdims=True) + EPS)
    return (yg.reshape(bsz, s, SSM_D_INNER) * norm_g.astype(f32)).astype(dtype)


def gated_delta_chunked(q, k, v, g, beta):
    bsz, s, h, dk = q.shape
    dv = v.shape[-1]
    c = s // CHUNK

    def chunks(t):
        return jnp.moveaxis(t.reshape((bsz, c, CHUNK) + t.shape[2:]), 3, 2)

    q = chunks(q) * (dk ** -0.5)
    k = chunks(k)
    v = chunks(v)
    g = chunks(g)
    beta = chunks(beta)
    gc = jnp.cumsum(g, axis=-1)
    lower = jnp.tril(jnp.ones((CHUNK, CHUNK), dtype=bool))
    strict = jnp.tril(jnp.ones((CHUNK, CHUNK), jnp.float32), -1)
    decay = jnp.exp(jnp.where(lower, gc[..., :, None] - gc[..., None, :], -jnp.inf))
    kb = k * beta[..., None]
    a_mat = jnp.einsum('bchid,bchjd->bchij', kb, k) * decay * strict
    t_mat = a_mat + jnp.eye(CHUNK, dtype=jnp.float32)
    rhs = jnp.concatenate([v * beta[..., None], kb * jnp.exp(gc)[..., None]], axis=-1)
    sol = lax.linalg.triangular_solve(t_mat, rhs, left_side=True, lower=True, unit_diagonal=True)
    u, w = jnp.split(sol, [dv], axis=-1)
    qk = jnp.einsum('bchid,bchjd->bchij', q, k) * decay
    q_dec = q * jnp.exp(gc)[..., None]
    k_dec = k * jnp.exp(gc[..., -1:] - gc)[..., None]
    last = jnp.exp(gc[..., -1])

    def step(state, inp):
        u_c, w_c, qk_c, qd_c, kd_c, last_c = inp
        v_new = u_c - jnp.einsum('bhld,bhdv->bhlv', w_c, state)
        o = jnp.einsum('bhld,bhdv->bhlv', qd_c, state) + jnp.einsum('bhij,bhjv->bhiv', qk_c, v_new)
        state = state * last_c[..., None, None] + jnp.einsum('bhld,bhlv->bhdv', kd_c, v_new)
        return state, o

    xs = tuple(jnp.moveaxis(t, 1, 0) for t in (u, w, qk, q_dec, k_dec, last))
    _, o = lax.scan(step, jnp.zeros((bsz, h, dk, dv), jnp.float32), xs)
    return jnp.transpose(o, (1, 0, 3, 2, 4)).reshape(bsz, s, h, dv)


def gdn_branch(qkv, gate, b_raw, a_raw, conv_w, dt_bias, a_log, norm_g):
    dtype = qkv.dtype
    f32 = jnp.float32
    qkv = jax.nn.silu(causal_dwconv(qkv, conv_w)).astype(f32)
    q, k, v = jnp.split(qkv, [GDN_KEY, 2 * GDN_KEY], axis=-1)
    bsz, s, _ = q.shape
    q = l2_normalize(q.reshape(bsz, s, GDN_HEADS, GDN_DK))
    k = l2_normalize(k.reshape(bsz, s, GDN_HEADS, GDN_DK))
    v = v.reshape(bsz, s, GDN_HEADS, GDN_DV)
    beta = jax.nn.sigmoid(b_raw.astype(f32))
    g = -jnp.exp(a_log.astype(f32)) * jax.nn.softplus(a_raw.astype(f32) + dt_bias.astype(f32))
    o = gated_delta_chunked(q, k, v, g, beta)
    o = o * lax.rsqrt(jnp.mean(o * o, axis=-1, keepdims=True) + EPS) * norm_g.astype(f32)
    o = o.reshape(bsz, s, GDN_VAL) * jax.nn.silu(gate.astype(f32))
    return o.astype(dtype)


def setup_inputs(seed: int = 0) -> dict:
    key = jax.random.key(seed)
    ks = iter(jax.random.split(key, 48))
    f32 = jnp.float32

    def nrm(shape, scale):
        return jax.random.normal(next(ks), shape, f32) * scale

    def gain(n):
        return 1.0 + nrm((DEPTH, n), 0.02)

    def dt_bias(n):
        u = jax.random.uniform(next(ks), (DEPTH, n), f32)
        dt = jnp.exp(u * (math.log(0.1) - math.log(1e-3)) + math.log(1e-3))
        return dt + jnp.log(-jnp.expm1(-dt))

    def a_log(n):
        return jnp.log(jax.random.uniform(next(ks), (DEPTH, n), f32, 1.0, 16.0))

    return {
        'x': nrm((BATCH, SEQ, D_MODEL), 1.0),
        'p': nrm((DEPTH, BATCH, SEQ, PLE_DIM), 1.0),
        'g_mix_pre': gain(D_MODEL),
        'w_in': nrm((DEPTH, D_MODEL, D_IN), D_MODEL ** -0.5),
        'conv_ssm_w': nrm((DEPTH, SSM_CONV, SSM_XBC), SSM_CONV ** -0.5),
        'conv_ssm_b': nrm((DEPTH, SSM_XBC), 0.02),
        'ssm_dt_bias': dt_bias(SSM_HEADS),
        'ssm_a_log': a_log(SSM_HEADS),
        'ssm_d': 1.0 + nrm((DEPTH, SSM_HEADS), 0.02),
        'ssm_norm_g': gain(SSM_D_INNER),
        'conv_gdn_w': nrm((DEPTH, GDN_CONV, GDN_QKV), GDN_CONV ** -0.5),
        'gdn_dt_bias': dt_bias(GDN_HEADS),
        'gdn_a_log': a_log(GDN_HEADS),
        'gdn_norm_g': gain(GDN_DV),
        'w_br_ssm': nrm((DEPTH, SSM_D_INNER, D_MODEL), SSM_D_INNER ** -0.5),
        'w_br_gdn': nrm((DEPTH, GDN_VAL, D_MODEL), GDN_VAL ** -0.5),
        'w_out': nrm((DEPTH, D_MODEL, D_MODEL), D_MODEL ** -0.5),
        'g_mix_post': gain(D_MODEL),
        'g_ffn_pre': gain(D_MODEL),
        'w_ffn_gate': nrm((DEPTH, D_MODEL, D_FF), D_MODEL ** -0.5),
        'w_ffn_up': nrm((DEPTH, D_MODEL, D_FF), D_MODEL ** -0.5),
        'conv_ffn_w': nrm((DEPTH, FFN_CONV, D_FF), FFN_CONV ** -0.5),
        'conv_ffn_b': nrm((DEPTH, D_FF), 0.02),
        'w_ffn_down': nrm((DEPTH, D_FF, D_MODEL), D_FF ** -0.5),
        'g_ffn_post': gain(D_MODEL),
        'w_ple_gate': nrm((DEPTH, D_MODEL, D_MODEL), D_MODEL ** -0.5),
        'w_ple_proj': nrm((DEPTH, PLE_DIM, D_MODEL), PLE_DIM ** -0.5),
        'g_ple_post': gain(D_MODEL),
    }


def reference(x, p, g_mix_pre, w_in, conv_ssm_w, conv_ssm_b, ssm_dt_bias, ssm_a_log, ssm_d, ssm_norm_g,
              conv_gdn_w, gdn_dt_bias, gdn_a_log, gdn_norm_g, w_br_ssm, w_br_gdn, w_out, g_mix_post,
              g_ffn_pre, w_ffn_gate, w_ffn_up, conv_ffn_w, conv_ffn_b, w_ffn_down, g_ffn_post,
              w_ple_gate, w_ple_proj, g_ple_post):
    for i in range(DEPTH):
        h = rms_norm(x, g_mix_pre[i])
        proj = h @ w_in[i]
        z, xbc, dt_raw, qkv, gate, b_raw, a_raw, m_gate = jnp.split(proj, IN_SPLITS, axis=-1)
        y_ssm = mamba2_branch(z, xbc, dt_raw, conv_ssm_w[i], conv_ssm_b[i], ssm_dt_bias[i],
                              ssm_a_log[i], ssm_d[i], ssm_norm_g[i])
        y_gdn = gdn_branch(qkv, gate, b_raw, a_raw, conv_gdn_w[i], gdn_dt_bias[i], gdn_a_log[i],
                           gdn_norm_g[i])
        gate_ssm, gate_gdn = jnp.split(jax.nn.sigmoid(m_gate), 2, axis=-1)
        mixed = gate_ssm * (y_ssm @ w_br_ssm[i]) + gate_gdn * (y_gdn @ w_br_gdn[i])
        x = x + rms_norm(mixed @ w_out[i], g_mix_post[i])
        h = rms_norm(x, g_ffn_pre[i])
        act = jax.nn.gelu(causal_dwconv(h @ w_ffn_gate[i], conv_ffn_w[i]) + conv_ffn_b[i], approximate=True)
        f = (act * (h @ w_ffn_up[i])) @ w_ffn_down[i]
        x = x + rms_norm(f, g_ffn_post[i])
        e = jax.nn.sigmoid(x @ w_ple_gate[i]) * (p[i].astype(x.dtype) @ w_ple_proj[i])
        x = x + rms_norm(e, g_ple_post[i])
    return x
```

```python
import functools

import jax
import jax.numpy as jnp
from jax import lax
from jax.experimental import pallas as pl
from jax.experimental.pallas import tpu as pltpu

F32 = jnp.float32
BF16 = jnp.bfloat16
EPS = 1e-6

D_MODEL = 1024
CHUNK = 64
SSM_HEADS = 16
SSM_HEAD_DIM = 64
SSM_D_INNER = 1024
SSM_GROUPS = 2
SSM_STATE = 128
SSM_CONV = 4
GDN_HEADS = 8
GDN_DK = 128
GDN_DV = 128
GDN_CONV = 4
D_FF = 4096
FFN_CONV = 3
PLE_DIM = 256

LANES = 128
HALO = 8
FFN_HALO = 16
VMEM_LIMIT = 48 * 1024 * 1024

COL_Q, COL_K, COL_V, COL_XS, COL_Z, COL_GATE, COL_MGS, COL_MGG, COL_BC = (
    0, 1024, 2048, 3072, 4096, 5120, 6144, 7168, 8192)
PROJ_COLS = 8704
SM_DT, SM_B, SM_A = 0, 16, 24


def _rms(x, g):
    return x * lax.rsqrt(jnp.mean(x * x, axis=-1, keepdims=True) + EPS) * g


def _silu(x):
    return x * jax.nn.sigmoid(x)


def _softplus(x):
    return jnp.maximum(x, 0.0) + jnp.log1p(jnp.exp(-jnp.abs(x)))


def _gelu_tanh(x):
    c = 0.7978845608028654
    return 0.5 * x * (1.0 + jnp.tanh(c * (x + 0.044715 * (x * x * x))))


def _split2(v):
    hi = v.astype(BF16)
    lo = (v - hi.astype(F32)).astype(BF16)
    return hi, lo


def _split3(v):
    hi = v.astype(BF16)
    r = v - hi.astype(F32)
    mid = r.astype(BF16)
    lo = (r - mid.astype(F32)).astype(BF16)
    return hi, mid, lo


def _dot(a, b):
    return jnp.dot(a, b, preferred_element_type=F32)


def _dot_tb(a, b):
    return lax.dot_general(a, b, (((1,), (1,)), ((), ())), preferred_element_type=F32)


def _dot_ta(a, b):
    return lax.dot_general(a, b, (((0,), (0,)), ((), ())), preferred_element_type=F32)


def _sel_dot(sel, v, parts):
    ps = _split3(v) if parts == 3 else _split2(v)
    out = None
    for p in ps:
        t = _dot(sel, p)
        out = t if out is None else out + t
    return out


def _expand(v, e):
    hi, lo = _split2(v)
    return _dot(hi, e) + _dot(lo, e)


def _conv_rows(ext_ref, w_ref, r0, rows, lanes, taps, halo):
    acc = None
    for k in range(taps):
        off = halo - (taps - 1) + k + r0
        t = w_ref[k:k + 1, lanes] * ext_ref[off:off + rows, lanes]
        acc = t if acc is None else acc + t
    return acc


def _inproj_body(x_ref, g_ref, w_ref, ws_ref, o_ref, os_ref, h_ref):
    @pl.when(pl.program_id(1) == 0)
    def _():
        hb = _rms(x_ref[...], g_ref[...]).astype(BF16)
        h_ref[...] = hb
        os_ref[...] = _dot(hb, ws_ref[...])

    o_ref[...] = _dot(h_ref[...], w_ref[...]).astype(o_ref.dtype)


def _inproj(x, g, w_big, w_small, proj_dtype, tm=1024, tn=512):
    n = x.shape[0]
    return pl.pallas_call(
        _inproj_body,
        grid=(n // tm, PROJ_COLS // tn),
        in_specs=[
            pl.BlockSpec((tm, D_MODEL), lambda i, j: (i, 0)),
            pl.BlockSpec((1, D_MODEL), lambda i, j: (0, 0)),
            pl.BlockSpec((D_MODEL, tn), lambda i, j: (0, j)),
            pl.BlockSpec((D_MODEL, LANES), lambda i, j: (0, 0)),
        ],
        out_specs=[
            pl.BlockSpec((tm, tn), lambda i, j: (i, j)),
            pl.BlockSpec((tm, LANES), lambda i, j: (i, 0)),
        ],
        out_shape=[
            jax.ShapeDtypeStruct((n, PROJ_COLS), proj_dtype),
            jax.ShapeDtypeStruct((n, LANES), F32),
        ],
        scratch_shapes=[pltpu.VMEM((tm, D_MODEL), BF16)],
        compiler_params=pltpu.CompilerParams(
            dimension_semantics=("parallel", "arbitrary"), vmem_limit_bytes=VMEM_LIMIT),
        name="inproj",
    )(x, g, w_big, w_small)


def _ssd_body(xs_ref, bc_ref, z_ref, sm_ref, cwx_ref, cwbc_ref, cbx_ref, cbbc_ref, dtb_ref, alog_ref,
              dsk_ref, ng_ref, e_ref, o_ref, xext, bcext, state, *, rows_blk):
    q = CHUNK
    gw = SSM_D_INNER // SSM_GROUPS
    gn = SSM_STATE

    @pl.when(pl.program_id(1) == 0)
    def _():
        xext[0:HALO, :] = jnp.zeros((HALO, SSM_D_INNER), F32)
        bcext[0:HALO, :] = jnp.zeros((HALO, 2 * SSM_GROUPS * SSM_STATE), F32)
        state[...] = jnp.zeros(state.shape, F32)

    xext[HALO:HALO + rows_blk, :] = xs_ref[...].astype(F32)
    bcext[HALO:HALO + rows_blk, :] = bc_ref[...].astype(F32)

    lane = lax.broadcasted_iota(jnp.int32, (q, LANES), 1)
    rowi = lax.broadcasted_iota(jnp.int32, (q, LANES), 0)
    lo_half = lane < q
    causal2 = rowi >= (lane % q)
    li = lax.broadcasted_iota(jnp.int32, (q, q), 0)
    lj = lax.broadcasted_iota(jnp.int32, (q, q), 1)
    l_incl = (lj <= li).astype(BF16)
    u2 = (rowi <= (lane % q)).astype(BF16)
    r2 = lax.broadcasted_iota(jnp.int32, (2 * q, LANES), 0)
    c2 = lax.broadcasted_iota(jnp.int32, (2 * q, LANES), 1)
    blockdiag = (r2 < q) == (c2 < q)
    dt_lane = lax.broadcasted_iota(jnp.int32, (q, LANES), 1) < SSM_HEADS

    a_vec = -jnp.exp(alog_ref[...])
    e_mat = e_ref[...]
    all_lanes = slice(None)

    for c in range(rows_blk // q):
        r0 = c * q
        xcv = _silu(_conv_rows(xext, cwx_ref, r0, q, all_lanes, SSM_CONV, HALO) + cbx_ref[...])
        bcv = _silu(_conv_rows(bcext, cwbc_ref, r0, q, all_lanes, SSM_CONV, HALO) + cbbc_ref[...])
        dt = jnp.where(dt_lane, _softplus(sm_ref[r0:r0 + q, :] + dtb_ref[...]), 0.0)
        adt = dt * a_vec
        parts = _split3(adt)
        acum = _dot(l_incl, parts[0]) + _dot(l_incl, parts[1]) + _dot(l_incl, parts[2])
        acum_t = _dot_ta(parts[0], u2) + _dot_ta(parts[1], u2) + _dot_ta(parts[2], u2)
        ea = jnp.exp(acum)
        ds = jnp.exp(acum[q - 1:q, :] - acum)
        ex = _expand(jnp.concatenate([dt, ea, ds], axis=0), e_mat)
        dt_e, ea_e, ds_e = ex[0:q], ex[q:2 * q], ex[2 * q:3 * q]
        xdt = xcv * dt_e
        xds = (xdt * ds_e).astype(BF16)
        y_parts = []
        for g in range(SSM_GROUPS):
            bg = bcv[:, g * gn:(g + 1) * gn].astype(BF16)
            cg = bcv[:, SSM_GROUPS * gn + g * gn:SSM_GROUPS * gn + (g + 1) * gn].astype(BF16)
            sc2 = _dot_tb(cg, jnp.concatenate([bg, bg], axis=0))
            st = state[:, g * gw:(g + 1) * gw]
            yoff = _dot(cg, st.astype(BF16)) * ea_e[:, g * gw:(g + 1) * gw]
            for pr in range(gw // LANES):
                h0 = g * (SSM_HEADS // SSM_GROUPS) + 2 * pr
                l0 = g * gw + pr * LANES
                col = jnp.where(lo_half, acum[:, h0:h0 + 1], acum[:, h0 + 1:h0 + 2])
                row = jnp.where(lo_half, acum_t[h0:h0 + 1, :], acum_t[h0 + 1:h0 + 2, :])
                lm = jnp.exp(jnp.where(causal2, col - row, -jnp.inf))
                pm = (sc2 * lm).astype(BF16)
                xp = xdt[:, l0:l0 + LANES]
                rhs = jnp.where(blockdiag, jnp.concatenate([xp, xp], axis=0), 0.0).astype(BF16)
                y_parts.append(_dot(pm, rhs) + yoff[:, pr * LANES:(pr + 1) * LANES])
            contrib = _dot_ta(bg, xds[:, g * gw:(g + 1) * gw])
            state[:, g * gw:(g + 1) * gw] = st * ea_e[q - 1:q, g * gw:(g + 1) * gw] + contrib
        y = jnp.concatenate(y_parts, axis=1) + dsk_ref[...] * xcv
        yz = y * _silu(z_ref[r0:r0 + q, :].astype(F32))
        outs = []
        for g in range(SSM_GROUPS):
            part = yz[:, g * gw:(g + 1) * gw]
            outs.append(part * lax.rsqrt(jnp.mean(part * part, axis=-1, keepdims=True) + EPS))
        o_ref[r0:r0 + q, :] = (jnp.concatenate(outs, axis=1) * ng_ref[...]).astype(o_ref.dtype)

    xext[0:HALO, :] = xext[rows_blk:rows_blk + HALO, :]
    bcext[0:HALO, :] = bcext[rows_blk:rows_blk + HALO, :]


def _ssd(proj, small, cwx, cwbc, cbx, cbbc, dtb, alog, dsk, ng, e_mat, batch, seq, rows_blk=256):
    nt = seq // rows_blk
    n = batch * seq
    bcw = 2 * SSM_GROUPS * SSM_STATE
    row = lambda b, t: b * nt + t
    full = lambda shape: pl.BlockSpec(shape, lambda b, t: (0,) * len(shape))
    return pl.pallas_call(
        functools.partial(_ssd_body, rows_blk=rows_blk),
        grid=(batch, nt),
        in_specs=[
            pl.BlockSpec((rows_blk, SSM_D_INNER), lambda b, t: (row(b, t), COL_XS // SSM_D_INNER)),
            pl.BlockSpec((rows_blk, bcw), lambda b, t: (row(b, t), COL_BC // bcw)),
            pl.BlockSpec((rows_blk, SSM_D_INNER), lambda b, t: (row(b, t), COL_Z // SSM_D_INNER)),
            pl.BlockSpec((rows_blk, LANES), lambda b, t: (row(b, t), 0)),
            full((SSM_CONV, SSM_D_INNER)), full((SSM_CONV, bcw)),
            full((1, SSM_D_INNER)), full((1, bcw)),
            full((1, LANES)), full((1, LANES)),
            full((1, SSM_D_INNER)), full((1, SSM_D_INNER)),
            full((LANES, SSM_D_INNER)),
        ],
        out_specs=pl.BlockSpec((rows_blk, SSM_D_INNER), lambda b, t: (row(b, t), 0)),
        out_shape=jax.ShapeDtypeStruct((n, SSM_D_INNER), BF16),
        scratch_shapes=[
            pltpu.VMEM((HALO + rows_blk, SSM_D_INNER), F32),
            pltpu.VMEM((HALO + rows_blk, bcw), F32),
            pltpu.VMEM((SSM_STATE, SSM_D_INNER), F32),
        ],
        compiler_params=pltpu.CompilerParams(
            dimension_semantics=("parallel", "arbitrary"), vmem_limit_bytes=VMEM_LIMIT),
        name="ssd",
    )(proj, proj, proj, small, cwx, cwbc, cbx, cbbc, dtb, alog, dsk, ng, e_mat)


def _gdn_body(q_ref, k_ref, v_ref, gate_ref, sm_ref, cwq_ref, cwk_ref, cwv_ref, dtb_ref, alog_ref, ng_ref,
              eb_ref, eg_ref, o_ref, qext, kext, vext, s_ref, *, rows_blk):
    sup = 2 * CHUNK
    width = GDN_HEADS * GDN_DK

    @pl.when(pl.program_id(1) == 0)
    def _():
        zeros = jnp.zeros((HALO, width), F32)
        qext[0:HALO, :] = zeros
        kext[0:HALO, :] = zeros
        vext[0:HALO, :] = zeros
        s_ref[...] = jnp.zeros(s_ref.shape, F32)

    qext[HALO:HALO + rows_blk, :] = q_ref[...].astype(F32)
    kext[HALO:HALO + rows_blk, :] = k_ref[...].astype(F32)
    vext[HALO:HALO + rows_blk, :] = v_ref[...].astype(F32)

    ri = lax.broadcasted_iota(jnp.int32, (sup, sup), 0)
    ci = lax.broadcasted_iota(jnp.int32, (sup, sup), 1)
    same = (ri < CHUNK) == (ci < CHUNK)
    incl = same & (ci <= ri)
    strict = same & (ci < ri)
    l_bd = incl.astype(BF16)
    u_bd = (same & (ri <= ci)).astype(BF16)
    eye = (ri == ci).astype(F32)
    lane = lax.broadcasted_iota(jnp.int32, (sup, LANES), 1)
    g_lane = (lane >= SM_A) & (lane < SM_A + GDN_HEADS)
    first_chunk = lax.broadcasted_iota(jnp.int32, (sup, LANES), 0) < CHUNK

    a_vec = -jnp.exp(alog_ref[...])
    scale = GDN_DK ** -0.5

    for sc in range(rows_blk // sup):
        r0 = sc * sup
        sm = sm_ref[r0:r0 + sup, :]
        beta = jax.nn.sigmoid(sm)
        gl = jnp.where(g_lane, a_vec * _softplus(sm + dtb_ref[...]), 0.0)
        parts = _split3(gl)
        gc = _dot(l_bd, parts[0]) + _dot(l_bd, parts[1]) + _dot(l_bd, parts[2])
        gc_t = _dot_ta(parts[0], u_bd) + _dot_ta(parts[1], u_bd) + _dot_ta(parts[2], u_bd)
        eg = jnp.exp(gc)
        gc_last = jnp.where(first_chunk, gc[CHUNK - 1:CHUNK, :], gc[sup - 1:sup, :])
        ekd = jnp.exp(gc_last - gc)
        beta_e = _expand(beta, eb_ref[...])
        ex = _expand(jnp.concatenate([eg, ekd], axis=0), eg_ref[...])
        eg_e, ekd_e = ex[0:sup], ex[sup:2 * sup]

        for h in range(GDN_HEADS):
            ln = slice(h * GDN_DK, (h + 1) * GDN_DK)
            qh = _silu(_conv_rows(qext, cwq_ref, r0, sup, ln, GDN_CONV, HALO))
            kh = _silu(_conv_rows(kext, cwk_ref, r0, sup, ln, GDN_CONV, HALO))
            vh = _silu(_conv_rows(vext, cwv_ref, r0, sup, ln, GDN_CONV, HALO))
            qs = qh * (lax.rsqrt(jnp.sum(qh * qh, axis=-1, keepdims=True) + EPS) * scale)
            kn = kh * lax.rsqrt(jnp.sum(kh * kh, axis=-1, keepdims=True) + EPS)
            b_h = beta_e[:, ln]
            eg_h = eg_e[:, ln]
            kb = kn * b_h
            kq = _dot_tb(jnp.concatenate([kb, qs], axis=0).astype(BF16), kn.astype(BF16))
            col = gc[:, SM_A + h:SM_A + h + 1]
            row = gc_t[SM_A + h:SM_A + h + 1, :]
            dec = jnp.exp(jnp.where(incl, col - row, -jnp.inf))
            nmat = jnp.where(strict, -(kq[0:sup] * dec), 0.0)
            qk = kq[sup:2 * sup] * dec
            pmat = eye + nmat
            mpow = nmat
            for _ in range(5):
                mb = mpow.astype(BF16)
                mpow = _dot(mb, mb)
                pmat = pmat + _dot(pmat.astype(BF16), mpow.astype(BF16))
            rhs = jnp.concatenate([vh * b_h, kb * eg_h], axis=1).astype(BF16)
            uw = _dot(pmat.astype(BF16), rhs)
            u, w = uw[:, 0:GDN_DV], uw[:, GDN_DV:2 * GDN_DV]
            qd = qs * eg_h
            kd = (kn * ekd_e[:, ln]).astype(BF16)

            s = s_ref[h]
            vnew, o_s = [], []
            for c in range(2):
                rows = slice(c * CHUNK, (c + 1) * CHUNK)
                wq = jnp.concatenate([w[rows], qd[rows]], axis=0).astype(BF16)
                r = _dot(wq, s.astype(BF16))
                vn = u[rows] - r[0:CHUNK]
                vnew.append(vn)
                o_s.append(r[CHUNK:2 * CHUNK])
                last = eg_h[(c + 1) * CHUNK - 1:(c + 1) * CHUNK, :]
                s = s * last + _dot_ta(kd[rows], vn.astype(BF16))
            s_ref[h] = s
            o = jnp.concatenate(o_s, axis=0) + _dot(qk.astype(BF16), jnp.concatenate(vnew, axis=0).astype(BF16))
            o = o * lax.rsqrt(jnp.mean(o * o, axis=-1, keepdims=True) + EPS) * ng_ref[...]
            o = o * _silu(gate_ref[r0:r0 + sup, ln].astype(F32))
            o_ref[r0:r0 + sup, ln] = o.astype(o_ref.dtype)

    qext[0:HALO, :] = qext[rows_blk:rows_blk + HALO, :]
    kext[0:HALO, :] = kext[rows_blk:rows_blk + HALO, :]
    vext[0:HALO, :] = vext[rows_blk:rows_blk + HALO, :]


def _gdn(proj, small, cwq, cwk, cwv, dtb, alog, ng, eb, eg, batch, seq, rows_blk=256):
    nt = seq // rows_blk
    n = batch * seq
    width = GDN_HEADS * GDN_DK
    row = lambda b, t: b * nt + t
    full = lambda shape: pl.BlockSpec(shape, lambda b, t: (0,) * len(shape))
    colblk = lambda col: pl.BlockSpec((rows_blk, width), lambda b, t: (row(b, t), col // width))
    return pl.pallas_call(
        functools.partial(_gdn_body, rows_blk=rows_blk),
        grid=(batch, nt),
        in_specs=[
            colblk(COL_Q), colblk(COL_K), colblk(COL_V), colblk(COL_GATE),
            pl.BlockSpec((rows_blk, LANES), lambda b, t: (row(b, t), 0)),
            full((GDN_CONV, width)), full((GDN_CONV, width)), full((GDN_CONV, width)),
            full((1, LANES)), full((1, LANES)), full((1, GDN_DV)),
            full((LANES, width)), full((LANES, width)),
        ],
        out_specs=pl.BlockSpec((rows_blk, width), lambda b, t: (row(b, t), 0)),
        out_shape=jax.ShapeDtypeStruct((n, width), BF16),
        scratch_shapes=[
            pltpu.VMEM((HALO + rows_blk, width), F32),
            pltpu.VMEM((HALO + rows_blk, width), F32),
            pltpu.VMEM((HALO + rows_blk, width), F32),
            pltpu.VMEM((GDN_HEADS, GDN_DK, GDN_DV), F32),
        ],
        compiler_params=pltpu.CompilerParams(
            dimension_semantics=("parallel", "arbitrary"), vmem_limit_bytes=VMEM_LIMIT),
        name="gdn",
    )(proj, proj, proj, proj, small, cwq, cwk, cwv, dtb, alog, ng, eb, eg)


def _mix_body(x_ref, ys_ref, yg_ref, ms_ref, mg_ref, ws_ref, wg_ref, wo_ref, g_ref, o_ref):
    a = _dot(ys_ref[...], ws_ref[...])
    b = _dot(yg_ref[...], wg_ref[...])
    mixed = jax.nn.sigmoid(ms_ref[...].astype(F32)) * a + jax.nn.sigmoid(mg_ref[...].astype(F32)) * b
    out = _dot(mixed.astype(BF16), wo_ref[...])
    o_ref[...] = x_ref[...] + _rms(out, g_ref[...])


def _mix(x, y_ssm, y_gdn, proj, w_s, w_g, w_o, g, tm=512):
    n = x.shape[0]
    rowblk = lambda col: pl.BlockSpec((tm, D_MODEL), lambda i: (i, col // D_MODEL))
    full = lambda shape: pl.BlockSpec(shape, lambda i: (0,) * len(shape))
    return pl.pallas_call(
        _mix_body,
        grid=(n // tm,),
        in_specs=[rowblk(0), rowblk(0), rowblk(0), rowblk(COL_MGS), rowblk(COL_MGG),
                  full((D_MODEL, D_MODEL)), full((D_MODEL, D_MODEL)), full((D_MODEL, D_MODEL)),
                  full((1, D_MODEL))],
        out_specs=rowblk(0),
        out_shape=jax.ShapeDtypeStruct((n, D_MODEL), F32),
        compiler_params=pltpu.CompilerParams(
            dimension_semantics=("parallel",), vmem_limit_bytes=VMEM_LIMIT),
        name="mix",
    )(x, y_ssm, y_gdn, proj, proj, w_s, w_g, w_o, g)


def _ffn_body(x_ref, xp_ref, gpre_ref, wg_ref, wu_ref, wd_ref, cw_ref, cb_ref, gpost_ref, o_ref,
              h_ref, gbuf, acc_ref, *, tm, tiles_per_seq):
    i = pl.program_id(0)
    j = pl.program_id(1)

    @pl.when(j == 0)
    def _():
        h_ref[FFN_HALO:FFN_HALO + tm, :] = _rms(x_ref[...], gpre_ref[...]).astype(BF16)
        h_ref[0:FFN_HALO, :] = _rms(xp_ref[...], gpre_ref[...]).astype(BF16)
        acc_ref[...] = jnp.zeros(acc_ref.shape, F32)

    gbuf[...] = _dot(h_ref[...], wg_ref[...])

    @pl.when(i % tiles_per_seq == 0)
    def _():
        gbuf[0:FFN_HALO, :] = jnp.zeros((FFN_HALO, gbuf.shape[1]), F32)

    conv = _conv_rows(gbuf, cw_ref, 0, tm, slice(None), FFN_CONV, FFN_HALO) + cb_ref[...]
    up = _dot(h_ref[FFN_HALO:FFN_HALO + tm, :], wu_ref[...])
    acc_ref[...] += _dot((_gelu_tanh(conv) * up).astype(BF16), wd_ref[...])

    @pl.when(j == pl.num_programs(1) - 1)
    def _():
        o_ref[...] = x_ref[...] + _rms(acc_ref[...], gpost_ref[...])


def _ffn(x, gpre, w_gate, w_up, w_down, cw, cb, gpost, seq, tm=1024, tf=512):
    n = x.shape[0]
    hb = tm // FFN_HALO
    return pl.pallas_call(
        functools.partial(_ffn_body, tm=tm, tiles_per_seq=seq // tm),
        grid=(n // tm, D_FF // tf),
        in_specs=[
            pl.BlockSpec((tm, D_MODEL), lambda i, j: (i, 0)),
            pl.BlockSpec((FFN_HALO, D_MODEL), lambda i, j: (jnp.maximum(i * hb - 1, 0), 0)),
            pl.BlockSpec((1, D_MODEL), lambda i, j: (0, 0)),
            pl.BlockSpec((D_MODEL, tf), lambda i, j: (0, j)),
            pl.BlockSpec((D_MODEL, tf), lambda i, j: (0, j)),
            pl.BlockSpec((tf, D_MODEL), lambda i, j: (j, 0)),
            pl.BlockSpec((FFN_CONV, tf), lambda i, j: (0, j)),
            pl.BlockSpec((1, tf), lambda i, j: (0, j)),
            pl.BlockSpec((1, D_MODEL), lambda i, j: (0, 0)),
        ],
        out_specs=pl.BlockSpec((tm, D_MODEL), lambda i, j: (i, 0)),
        out_shape=jax.ShapeDtypeStruct((n, D_MODEL), F32),
        scratch_shapes=[
            pltpu.VMEM((FFN_HALO + tm, D_MODEL), BF16),
            pltpu.VMEM((FFN_HALO + tm, tf), F32),
            pltpu.VMEM((tm, D_MODEL), F32),
        ],
        compiler_params=pltpu.CompilerParams(
            dimension_semantics=("parallel", "arbitrary"), vmem_limit_bytes=VMEM_LIMIT),
        name="ffn",
    )(x, x, gpre, w_gate, w_up, w_down, cw, cb, gpost)


def _ple_body(x_ref, p_ref, wg_ref, wp_ref, g_ref, o_ref):
    x = x_ref[...]
    e = jax.nn.sigmoid(_dot(x.astype(BF16), wg_ref[...])) * _dot(p_ref[...].astype(BF16), wp_ref[...])
    o_ref[...] = x + _rms(e, g_ref[...])


def _ple(x, p_all, layer, w_gate, w_proj, g, tm=512):
    n = x.shape[0]
    return pl.pallas_call(
        _ple_body,
        grid=(n // tm,),
        in_specs=[
            pl.BlockSpec((tm, D_MODEL), lambda i: (i, 0)),
            pl.BlockSpec((None, tm, PLE_DIM), lambda i: (layer, i, 0)),
            pl.BlockSpec((D_MODEL, D_MODEL), lambda i: (0, 0)),
            pl.BlockSpec((PLE_DIM, D_MODEL), lambda i: (0, 0)),
            pl.BlockSpec((1, D_MODEL), lambda i: (0, 0)),
        ],
        out_specs=pl.BlockSpec((tm, D_MODEL), lambda i: (i, 0)),
        out_shape=jax.ShapeDtypeStruct((n, D_MODEL), F32),
        compiler_params=pltpu.CompilerParams(
            dimension_semantics=("parallel",), vmem_limit_bytes=VMEM_LIMIT),
        name="ple",
    )(x, p_all, w_gate, w_proj, g)


def _pad_lanes(v, offset):
    return jnp.zeros((1, LANES), F32).at[0, offset:offset + v.shape[0]].set(v.astype(F32))


def _expansion(offset, heads, width):
    r = jnp.arange(LANES)[:, None]
    c = jnp.arange(heads * width)[None, :]
    return (r == offset + c // width).astype(BF16)


def _split_w_in(w):
    i0 = 0
    z = w[:, i0:i0 + 1024]; i0 += 1024
    xs = w[:, i0:i0 + 1024]; i0 += 1024
    bc = w[:, i0:i0 + 512]; i0 += 512
    dt = w[:, i0:i0 + 16]; i0 += 16
    qkv = w[:, i0:i0 + 3072]; i0 += 3072
    gate = w[:, i0:i0 + 1024]; i0 += 1024
    b = w[:, i0:i0 + 8]; i0 += 8
    a = w[:, i0:i0 + 8]; i0 += 8
    mg = w[:, i0:i0 + 2048]
    big = jnp.concatenate([qkv, xs, z, gate, mg, bc], axis=1).astype(BF16)
    small = jnp.concatenate([dt, b, a, jnp.zeros((w.shape[0], LANES - 32), w.dtype)], axis=1).astype(BF16)
    return big, small


PROJ_DTYPE = F32


def kernel(x, p, g_mix_pre, w_in, conv_ssm_w, conv_ssm_b, ssm_dt_bias, ssm_a_log, ssm_d, ssm_norm_g,
           conv_gdn_w, gdn_dt_bias, gdn_a_log, gdn_norm_g, w_br_ssm, w_br_gdn, w_out, g_mix_post,
           g_ffn_pre, w_ffn_gate, w_ffn_up, conv_ffn_w, conv_ffn_b, w_ffn_down, g_ffn_post,
           w_ple_gate, w_ple_proj, g_ple_post):
    batch, seq, d = x.shape
    depth = w_in.shape[0]
    n = batch * seq
    assert d == D_MODEL and seq % 1024 == 0
    xf = x.reshape(n, d)
    p_all = p.reshape(depth, n, PLE_DIM)
    row = lambda v: v.astype(F32).reshape(1, -1)

    e_ssd = _expansion(SM_DT, SSM_HEADS, SSM_HEAD_DIM)
    e_beta = _expansion(SM_B, GDN_HEADS, GDN_DK)
    e_dec = _expansion(SM_A, GDN_HEADS, GDN_DK)

    for i in range(depth):
        w_big, w_small = _split_w_in(w_in[i])
        proj, small = _inproj(xf, row(g_mix_pre[i]), w_big, w_small, PROJ_DTYPE)

        cw = conv_ssm_w[i].astype(F32)
        cb = conv_ssm_b[i].astype(F32)
        y_ssm = _ssd(
            proj, small, cw[:, :SSM_D_INNER], cw[:, SSM_D_INNER:], row(cb[:SSM_D_INNER]), row(cb[SSM_D_INNER:]),
            _pad_lanes(ssm_dt_bias[i], SM_DT), _pad_lanes(ssm_a_log[i], SM_DT),
            row(jnp.repeat(ssm_d[i], SSM_HEAD_DIM)), row(ssm_norm_g[i]), e_ssd, batch, seq)

        cg = conv_gdn_w[i].astype(F32)
        y_gdn = _gdn(
            proj, small, cg[:, 0:1024], cg[:, 1024:2048], cg[:, 2048:3072],
            _pad_lanes(gdn_dt_bias[i], SM_A), _pad_lanes(gdn_a_log[i], SM_A), row(gdn_norm_g[i]),
            e_beta, e_dec, batch, seq)

        xf = _mix(xf, y_ssm, y_gdn, proj, w_br_ssm[i].astype(BF16), w_br_gdn[i].astype(BF16),
                  w_out[i].astype(BF16), row(g_mix_post[i]))
        xf = _ffn(xf, row(g_ffn_pre[i]), w_ffn_gate[i].astype(BF16), w_ffn_up[i].astype(BF16),
                  w_ffn_down[i].astype(BF16), conv_ffn_w[i].astype(F32), row(conv_ffn_b[i]),
                  row(g_ffn_post[i]), seq)
        xf = _ple(xf, p_all, i, w_ple_gate[i].astype(BF16), w_ple_proj[i].astype(BF16), row(g_ple_post[i]))
    return xf.reshape(batch, seq, d)
```

```python
import functools

import jax
import jax.numpy as jnp
from jax import lax
from jax.experimental import pallas as pl
from jax.experimental.pallas import tpu as pltpu

F32 = jnp.float32
BF16 = jnp.bfloat16
EPS = 1e-6

D_MODEL = 1024
CHUNK = 64
SSM_HEADS = 16
SSM_HEAD_DIM = 64
SSM_D_INNER = 1024
SSM_GROUPS = 2
SSM_STATE = 128
SSM_CONV = 4
GDN_HEADS = 8
GDN_DK = 128
GDN_DV = 128
GDN_CONV = 4
D_FF = 4096
FFN_CONV = 3
PLE_DIM = 256

LANES = 128
SUBLANES = 8
MXU_COLS = 256
CONV_STRIDE = 4
FFN_HALO = 16
VMEM_LIMIT = 48 * 1024 * 1024

COL_Q, COL_K, COL_V, COL_XS, COL_Z, COL_GATE, COL_MGS, COL_MGG, COL_BC = (
    0, 1024, 2048, 3072, 4096, 5120, 6144, 7168, 8192)
PROJ_COLS = 8704
SM_DT, SM_B, SM_A = 0, 16, 24


def _rms(x, g):
    return x * lax.rsqrt(jnp.mean(x * x, axis=-1, keepdims=True) + EPS) * g


def _silu(x):
    return x * jax.nn.sigmoid(x)


def _softplus(x):
    return jnp.maximum(x, 0.0) + jnp.log1p(jnp.exp(-jnp.abs(x)))


def _gelu_tanh(x):
    c = 0.7978845608028654
    return 0.5 * x * (1.0 + jnp.tanh(c * (x + 0.044715 * (x * x * x))))


def _split2(v):
    hi = v.astype(BF16)
    lo = (v - hi.astype(F32)).astype(BF16)
    return hi, lo


def _split3(v):
    hi = v.astype(BF16)
    r = v - hi.astype(F32)
    mid = r.astype(BF16)
    lo = (r - mid.astype(F32)).astype(BF16)
    return hi, mid, lo


def _dot(a, b):
    return jnp.dot(a, b, preferred_element_type=F32)


def _dot_tb(a, b):
    return lax.dot_general(a, b, (((1,), (1,)), ((), ())), preferred_element_type=F32)


def _dot_ta(a, b):
    return lax.dot_general(a, b, (((0,), (0,)), ((), ())), preferred_element_type=F32)


def _expand(v, e):
    hi, lo = _split2(v)
    return _dot(hi, e) + _dot(lo, e)


def _conv_silu_tiles(x_ref, tail_ref, w_ref, b_ref, out_ref, r0, rows, taps):
    st = CONV_STRIDE
    assert taps - 1 <= st <= SUBLANES and rows % (SUBLANES * st) == 0 and r0 % (SUBLANES * st) == 0
    first_sublane = lax.broadcasted_iota(jnp.int32, (SUBLANES, LANES), 0) == 0
    for lt in range(x_ref.shape[0]):
        ln = slice(lt * LANES, (lt + 1) * LANES)
        w = [jnp.broadcast_to(w_ref[k:k + 1, ln], (SUBLANES, LANES)) for k in range(taps)]
        bias = None if b_ref is None else jnp.broadcast_to(b_ref[:, ln], (SUBLANES, LANES))
        for a in range(r0, r0 + rows, SUBLANES * st):
            v = [x_ref[lt, pl.ds(a + j, SUBLANES, stride=st), :] for j in range(st)]
            hist = []
            for d in range(1, taps):
                before = tail_ref[lt, SUBLANES - d:SUBLANES - d + 1, :] if a == 0 else x_ref[lt, a - d:a - d + 1, :]
                hist.append(jnp.where(first_sublane, before, pltpu.roll(v[st - d], 1, 0)))
            for j in range(st):
                acc = w[taps - 1] * v[j]
                for d in range(1, taps):
                    acc = acc + w[taps - 1 - d] * (v[j - d] if j >= d else hist[d - j - 1])
                if bias is not None:
                    acc = acc + bias
                out_ref[lt, pl.ds(a + j, SUBLANES, stride=st), :] = _silu(acc)


def _save_tail(x_ref, tail_ref):
    rows = x_ref.shape[1]
    for lt in range(x_ref.shape[0]):
        tail_ref[lt] = x_ref[lt, rows - SUBLANES:rows, :]


def _lanes(ref, r0, rows):
    return jnp.concatenate([ref[lt, r0:r0 + rows, :] for lt in range(ref.shape[0])], axis=1)


def _conv_rows(ext_ref, w_ref, rows, lanes, taps, halo):
    acc = None
    for k in range(taps):
        off = halo - (taps - 1) + k
        t = w_ref[k:k + 1, lanes] * ext_ref[off:off + rows, lanes]
        acc = t if acc is None else acc + t
    return acc


def _spread(main, extra):
    out = list(main)
    for k, item in enumerate(extra):
        out.insert((k + 1) * len(main) // (len(extra) + 1) + k, item)
    return out


def _emit(groups, side):
    done = 0
    for gi, grp in enumerate(groups):
        grp()
        upto = (gi + 1) * len(side) // len(groups)
        for item in side[done:upto]:
            item()
        done = upto


def _inproj_body(x_ref, g_ref, w_ref, ws_ref, o_ref, os_ref, h_ref):
    @pl.when(pl.program_id(1) == 0)
    def _():
        hb = _rms(x_ref[...], g_ref[...]).astype(BF16)
        h_ref[...] = hb
        os_ref[...] = _dot(hb, ws_ref[...])

    res = _dot(h_ref[...], w_ref[...]).astype(o_ref.dtype)
    for lt in range(o_ref.shape[0]):
        o_ref[lt] = res[:, lt * LANES:(lt + 1) * LANES]


def _inproj(x, g, w_big, w_small, proj_dtype, tm=1024, tn=2176):
    n = x.shape[0]
    return pl.pallas_call(
        _inproj_body,
        grid=(n // tm, PROJ_COLS // tn),
        in_specs=[
            pl.BlockSpec((tm, D_MODEL), lambda i, j: (i, 0)),
            pl.BlockSpec((1, D_MODEL), lambda i, j: (0, 0)),
            pl.BlockSpec((D_MODEL, tn), lambda i, j: (0, j)),
            pl.BlockSpec((D_MODEL, LANES), lambda i, j: (0, 0)),
        ],
        out_specs=[
            pl.BlockSpec((tn // LANES, tm, LANES), lambda i, j: (j, i, 0)),
            pl.BlockSpec((tm, LANES), lambda i, j: (i, 0)),
        ],
        out_shape=[
            jax.ShapeDtypeStruct((PROJ_COLS // LANES, n, LANES), proj_dtype),
            jax.ShapeDtypeStruct((n, LANES), F32),
        ],
        scratch_shapes=[pltpu.VMEM((tm, D_MODEL), BF16)],
        compiler_params=pltpu.CompilerParams(
            dimension_semantics=("parallel", "arbitrary"), vmem_limit_bytes=VMEM_LIMIT),
        name="inproj",
    )(x, g, w_big, w_small)


def _ssd_body(xs_ref, bc_ref, z_ref, sm_ref, cwx_ref, cwbc_ref, cbx_ref, cbbc_ref, dtb_ref, alog_ref,
              dsk_ref, ng_ref, e_ref, o_ref, xtail, bctail, xc, bcc, state, *, rows_blk):
    q = CHUNK
    gw = SSM_D_INNER // SSM_GROUPS

    @pl.when(pl.program_id(1) == 0)
    def _():
        xtail[...] = jnp.zeros(xtail.shape, F32)
        bctail[...] = jnp.zeros(bctail.shape, F32)
        state[...] = jnp.zeros(state.shape, F32)

    _conv_silu_tiles(xs_ref, xtail, cwx_ref, cbx_ref, xc, 0, rows_blk, SSM_CONV)
    _conv_silu_tiles(bc_ref, bctail, cwbc_ref, cbbc_ref, bcc, 0, rows_blk, SSM_CONV)
    _save_tail(xs_ref, xtail)
    _save_tail(bc_ref, bctail)

    lane = lax.broadcasted_iota(jnp.int32, (q, LANES), 1)
    rowi = lax.broadcasted_iota(jnp.int32, (q, LANES), 0)
    lo_half = lane < q
    causal2 = rowi >= (lane % q)
    li = lax.broadcasted_iota(jnp.int32, (q, q), 0)
    lj = lax.broadcasted_iota(jnp.int32, (q, q), 1)
    l_incl = (lj <= li).astype(BF16)
    u2 = (rowi <= (lane % q)).astype(BF16)
    r2 = lax.broadcasted_iota(jnp.int32, (2 * q, LANES), 0)
    c2 = lax.broadcasted_iota(jnp.int32, (2 * q, LANES), 1)
    blockdiag = (r2 < q) == (c2 < q)
    dt_lane = lax.broadcasted_iota(jnp.int32, (q, LANES), 1) < SSM_HEADS

    a_vec = -jnp.exp(alog_ref[...])
    e_mat = e_ref[...]

    for c in range(rows_blk // q):
        r0 = c * q
        xcv = _lanes(xc, r0, q)
        dt = jnp.where(dt_lane, _softplus(sm_ref[r0:r0 + q, :] + dtb_ref[...]), 0.0)
        adt = dt * a_vec
        parts = _split3(adt)
        acum = _dot(l_incl, parts[0]) + _dot(l_incl, parts[1]) + _dot(l_incl, parts[2])
        acum_t = _dot_ta(parts[0], u2) + _dot_ta(parts[1], u2) + _dot_ta(parts[2], u2)
        ea = jnp.exp(acum)
        ds = jnp.exp(acum[q - 1:q, :] - acum)
        ex = _expand(jnp.concatenate([dt, ea, ds], axis=0), e_mat)
        dt_e, ea_e, ds_e = ex[0:q], ex[q:2 * q], ex[2 * q:3 * q]
        xdt = xcv * dt_e
        xds = (xdt * ds_e).astype(BF16)
        y_parts = []
        for g in range(SSM_GROUPS):
            bg = bcc[g, r0:r0 + q, :].astype(BF16)
            cg = bcc[SSM_GROUPS + g, r0:r0 + q, :].astype(BF16)
            sc2 = _dot_tb(cg, jnp.concatenate([bg, bg], axis=0))
            st = state[:, g * gw:(g + 1) * gw]
            yoff = _dot(cg, st.astype(BF16)) * ea_e[:, g * gw:(g + 1) * gw]
            for pr in range(gw // LANES):
                h0 = g * (SSM_HEADS // SSM_GROUPS) + 2 * pr
                l0 = g * gw + pr * LANES
                col = jnp.where(lo_half, acum[:, h0:h0 + 1], acum[:, h0 + 1:h0 + 2])
                row = jnp.where(lo_half, acum_t[h0:h0 + 1, :], acum_t[h0 + 1:h0 + 2, :])
                lm = jnp.exp(jnp.where(causal2, col - row, -jnp.inf))
                pm = (sc2 * lm).astype(BF16)
                xp = xdt[:, l0:l0 + LANES]
                rhs = jnp.where(blockdiag, jnp.concatenate([xp, xp], axis=0), 0.0).astype(BF16)
                y_parts.append(_dot(pm, rhs) + yoff[:, pr * LANES:(pr + 1) * LANES])
            contrib = _dot_ta(bg, xds[:, g * gw:(g + 1) * gw])
            state[:, g * gw:(g + 1) * gw] = st * ea_e[q - 1:q, g * gw:(g + 1) * gw] + contrib
        y = jnp.concatenate(y_parts, axis=1) + dsk_ref[...] * xcv
        yz = y * _silu(_lanes(z_ref, r0, q).astype(F32))
        outs = []
        for g in range(SSM_GROUPS):
            part = yz[:, g * gw:(g + 1) * gw]
            outs.append(part * lax.rsqrt(jnp.mean(part * part, axis=-1, keepdims=True) + EPS))
        o_ref[r0:r0 + q, :] = (jnp.concatenate(outs, axis=1) * ng_ref[...]).astype(o_ref.dtype)


def _ssd(proj, small, cwx, cwbc, cbx, cbbc, dtb, alog, dsk, ng, e_mat, batch, seq, rows_blk=256):
    nt = seq // rows_blk
    n = batch * seq
    bcw = 2 * SSM_GROUPS * SSM_STATE
    xt = SSM_D_INNER // LANES
    bt = bcw // LANES
    row = lambda b, t: b * nt + t
    full = lambda shape: pl.BlockSpec(shape, lambda b, t: (0,) * len(shape))
    return pl.pallas_call(
        functools.partial(_ssd_body, rows_blk=rows_blk),
        grid=(batch, nt),
        in_specs=[
            pl.BlockSpec((xt, rows_blk, LANES), lambda b, t: (COL_XS // SSM_D_INNER, row(b, t), 0)),
            pl.BlockSpec((bt, rows_blk, LANES), lambda b, t: (COL_BC // bcw, row(b, t), 0)),
            pl.BlockSpec((xt, rows_blk, LANES), lambda b, t: (COL_Z // SSM_D_INNER, row(b, t), 0)),
            pl.BlockSpec((rows_blk, LANES), lambda b, t: (row(b, t), 0)),
            full((SSM_CONV, SSM_D_INNER)), full((SSM_CONV, bcw)),
            full((1, SSM_D_INNER)), full((1, bcw)),
            full((1, LANES)), full((1, LANES)),
            full((1, SSM_D_INNER)), full((1, SSM_D_INNER)),
            full((LANES, SSM_D_INNER)),
        ],
        out_specs=pl.BlockSpec((rows_blk, SSM_D_INNER), lambda b, t: (row(b, t), 0)),
        out_shape=jax.ShapeDtypeStruct((n, SSM_D_INNER), BF16),
        scratch_shapes=[
            pltpu.VMEM((xt, SUBLANES, LANES), F32),
            pltpu.VMEM((bt, SUBLANES, LANES), F32),
            pltpu.VMEM((xt, rows_blk, LANES), F32),
            pltpu.VMEM((bt, rows_blk, LANES), F32),
            pltpu.VMEM((SSM_STATE, SSM_D_INNER), F32),
        ],
        compiler_params=pltpu.CompilerParams(
            dimension_semantics=("parallel", "arbitrary"), vmem_limit_bytes=VMEM_LIMIT),
        name="ssd",
    )(proj, proj, proj, small, cwx, cwbc, cbx, cbbc, dtb, alog, dsk, ng, e_mat)


def _gdn_body(q_ref, k_ref, v_ref, gate_ref, sm_ref, cwq_ref, cwk_ref, cwv_ref, dtb_ref, alog_ref, ng_ref,
              eb_ref, eg_ref, o_ref,
              qtail, ktail, vtail, qc, kc, vc, gc_ref, gct_ref, be_ref, ege_ref, ekde_ref,
              kbq_ref, kn_ref, dec_ref, m_ref, p_ref, qk_ref, rhs_ref, wq_ref, u_ref, kd_ref, vn_ref, os_ref,
              last_ref, s_ref,
              *, rows_blk):
    sup = 2 * CHUNK
    width = GDN_HEADS * GDN_DK
    nsc = rows_blk // sup
    nh = GDN_HEADS

    @pl.when(pl.program_id(1) == 0)
    def _():
        qtail[...] = jnp.zeros(qtail.shape, F32)
        ktail[...] = jnp.zeros(ktail.shape, F32)
        vtail[...] = jnp.zeros(vtail.shape, F32)
        s_ref[...] = jnp.zeros(s_ref.shape, F32)

    def conv(sc, stream):
        src, tail, cw, dst = ((q_ref, qtail, cwq_ref, qc), (k_ref, ktail, cwk_ref, kc),
                              (v_ref, vtail, cwv_ref, vc))[stream]
        _conv_silu_tiles(src, tail, cw, None, dst, sc * sup, sup, GDN_CONV)

    ri = lax.broadcasted_iota(jnp.int32, (sup, sup), 0)
    ci = lax.broadcasted_iota(jnp.int32, (sup, sup), 1)
    same = (ri < CHUNK) == (ci < CHUNK)
    incl = same & (ci <= ri)
    strict = same & (ci < ri)
    l_bd = incl.astype(BF16)
    u_bd = (same & (ri <= ci)).astype(BF16)
    eye = (ri == ci).astype(F32)
    lane = lax.broadcasted_iota(jnp.int32, (sup, LANES), 1)
    g_lane = (lane >= SM_A) & (lane < SM_A + GDN_HEADS)
    first_chunk = lax.broadcasted_iota(jnp.int32, (sup, LANES), 0) < CHUNK

    a_vec = -jnp.exp(alog_ref[...])
    scale = GDN_DK ** -0.5

    def pair(sc, h):
        return sc * nh + h

    def prep(sc):
        r0 = sc * sup
        sm = sm_ref[r0:r0 + sup, :]
        gl = jnp.where(g_lane, a_vec * _softplus(sm + dtb_ref[...]), 0.0)
        parts = _split3(gl)
        gc = _dot(l_bd, parts[0]) + _dot(l_bd, parts[1]) + _dot(l_bd, parts[2])
        gc_ref[...] = gc
        gct_ref[...] = _dot_ta(parts[0], u_bd) + _dot_ta(parts[1], u_bd) + _dot_ta(parts[2], u_bd)
        eg = jnp.exp(gc)
        gc_last = jnp.where(first_chunk, gc[CHUNK - 1:CHUNK, :], gc[sup - 1:sup, :])
        ekd = jnp.exp(gc_last - gc)
        be_ref[...] = _expand(jax.nn.sigmoid(sm), eb_ref[...])
        ex = _expand(jnp.concatenate([eg, ekd], axis=0), eg_ref[...])
        ege_ref[...] = ex[0:sup]
        ekde_ref[...] = ex[sup:2 * sup]
        for c in range(2):
            last_ref[2 * sc + c] = jnp.broadcast_to(ex[(c + 1) * CHUNK - 1:(c + 1) * CHUNK, :], (SUBLANES, width))

    def a_vector(sc, h):
        p = pair(sc, h)
        r0 = sc * sup
        ln = slice(h * GDN_DK, (h + 1) * GDN_DK)
        qh = qc[h, r0:r0 + sup, :]
        kh = kc[h, r0:r0 + sup, :]
        vh = vc[h, r0:r0 + sup, :]
        qs = qh * (lax.rsqrt(jnp.sum(qh * qh, axis=-1, keepdims=True) + EPS) * scale)
        kn = kh * lax.rsqrt(jnp.sum(kh * kh, axis=-1, keepdims=True) + EPS)
        b_h = be_ref[:, ln]
        eg_h = ege_ref[:, ln]
        kb = kn * b_h
        kbq_ref[p] = jnp.concatenate([kb, qs], axis=0).astype(BF16)
        kn_ref[p] = kn.astype(BF16)
        col = gc_ref[:, SM_A + h:SM_A + h + 1]
        row = gct_ref[SM_A + h:SM_A + h + 1, :]
        dec_ref[p] = jnp.exp(jnp.where(incl, col - row, -jnp.inf))
        rhs_ref[p] = jnp.concatenate([vh * b_h, kb * eg_h], axis=1).astype(BF16)
        qd = (qs * eg_h).astype(BF16)
        for c in range(2):
            wq_ref[2 * p + c, CHUNK:2 * CHUNK, :] = qd[c * CHUNK:(c + 1) * CHUNK]
        kd_ref[p] = (kn * ekde_ref[:, ln]).astype(BF16)

    def a_matmul(sc):
        for h in range(nh):
            p = pair(sc, h)
            kq = _dot_tb(kbq_ref[p], kn_ref[p])
            dec = dec_ref[p]
            nmat = jnp.where(strict, -(kq[0:sup] * dec), 0.0)
            m_ref[p] = nmat.astype(BF16)
            p_ref[p] = eye + nmat
            qk_ref[p] = (kq[sup:2 * sup] * dec).astype(BF16)

    def square(p):
        mb = m_ref[p]
        m_ref[p] = _dot(mb, mb).astype(BF16)

    def accumulate(p):
        pv = p_ref[p]
        p_ref[p] = pv + _dot(pv.astype(BF16), m_ref[p])

    def phase_b_groups(sc):
        groups = []
        for _ in range(5):
            groups.append(lambda: [square(pair(sc, h)) for h in range(nh)])
            groups.append(lambda: [accumulate(pair(sc, h)) for h in range(nh)])
        return groups

    def phase_c(sc):
        for h in range(nh):
            p = pair(sc, h)
            uw = _dot(p_ref[p].astype(BF16), rhs_ref[p])
            u_ref[p] = uw[:, 0:GDN_DV]
            w = uw[:, GDN_DV:2 * GDN_DV].astype(BF16)
            for c in range(2):
                wq_ref[2 * p + c, 0:CHUNK, :] = w[c * CHUNK:(c + 1) * CHUNK]

    def d_apply(sc, c):
        rows = slice(c * CHUNK, (c + 1) * CHUNK)
        for h in range(nh):
            p = pair(sc, h)
            r = _dot(wq_ref[2 * p + c], s_ref[h].astype(BF16))
            vn_ref[p, rows, :] = (u_ref[p, rows, :] - r[0:CHUNK]).astype(BF16)
            os_ref[p, rows, :] = r[CHUNK:2 * CHUNK]

    def d_update(sc, c):
        rows = slice(c * CHUNK, (c + 1) * CHUNK)
        for h in range(nh):
            p = pair(sc, h)
            ln = slice(h * GDN_DK, (h + 1) * GDN_DK)
            s_ref[h] = s_ref[h] * last_ref[2 * sc + c, 0:1, ln] + _dot_ta(kd_ref[p, rows, :], vn_ref[p, rows, :])

    def phase_d_items(sc):
        items = []
        for c in range(2):
            items.append(functools.partial(d_apply, sc, c))
            items.append(functools.partial(d_update, sc, c))
        return items

    def phase_e(sc, h):
        p = pair(sc, h)
        r0 = sc * sup
        ln = slice(h * GDN_DK, (h + 1) * GDN_DK)
        o = os_ref[p] + _dot(qk_ref[p], vn_ref[p])
        o = o * lax.rsqrt(jnp.mean(o * o, axis=-1, keepdims=True) + EPS) * ng_ref[...]
        o = o * _silu(gate_ref[h, r0:r0 + sup, :].astype(F32))
        o_ref[r0:r0 + sup, ln] = o.astype(o_ref.dtype)

    def phase_e_items(sc):
        return [functools.partial(phase_e, sc, h) for h in range(nh)]

    def a_vector_items(sc):
        return ([functools.partial(prep, sc)] + [functools.partial(conv, sc, s) for s in range(3)]
                + [functools.partial(a_vector, sc, h) for h in range(nh)])

    for item in a_vector_items(0):
        item()
    a_matmul(0)
    for sc in range(nsc):
        side = _spread(a_vector_items(sc + 1) if sc + 1 < nsc else [],
                       phase_d_items(sc - 1) if sc >= 1 else [])
        _emit(phase_b_groups(sc), side)
        phase_c(sc)
        if sc + 1 < nsc:
            a_matmul(sc + 1)
        if sc >= 1:
            for item in phase_e_items(sc - 1):
                item()
    for item in phase_d_items(nsc - 1) + phase_e_items(nsc - 1):
        item()
    _save_tail(q_ref, qtail)
    _save_tail(k_ref, ktail)
    _save_tail(v_ref, vtail)


def _gdn(proj, small, cwq, cwk, cwv, dtb, alog, ng, eb, eg, batch, seq, rows_blk=256):
    nt = seq // rows_blk
    n = batch * seq
    width = GDN_HEADS * GDN_DK
    sup = 2 * CHUNK
    npair = (rows_blk // sup) * GDN_HEADS
    row = lambda b, t: b * nt + t
    full = lambda shape: pl.BlockSpec(shape, lambda b, t: (0,) * len(shape))
    colblk = lambda col: pl.BlockSpec((GDN_HEADS, rows_blk, LANES), lambda b, t: (col // width, row(b, t), 0))
    return pl.pallas_call(
        functools.partial(_gdn_body, rows_blk=rows_blk),
        grid=(batch, nt),
        in_specs=[
            colblk(COL_Q), colblk(COL_K), colblk(COL_V), colblk(COL_GATE),
            pl.BlockSpec((rows_blk, LANES), lambda b, t: (row(b, t), 0)),
            full((GDN_CONV, width)), full((GDN_CONV, width)), full((GDN_CONV, width)),
            full((1, LANES)), full((1, LANES)), full((1, GDN_DV)),
            full((LANES, width)), full((LANES, width)),
        ],
        out_specs=pl.BlockSpec((rows_blk, width), lambda b, t: (row(b, t), 0)),
        out_shape=jax.ShapeDtypeStruct((n, width), BF16),
        scratch_shapes=[
            pltpu.VMEM((GDN_HEADS, SUBLANES, LANES), F32),
            pltpu.VMEM((GDN_HEADS, SUBLANES, LANES), F32),
            pltpu.VMEM((GDN_HEADS, SUBLANES, LANES), F32),
            pltpu.VMEM((GDN_HEADS, rows_blk, LANES), F32),
            pltpu.VMEM((GDN_HEADS, rows_blk, LANES), F32),
            pltpu.VMEM((GDN_HEADS, rows_blk, LANES), F32),
            pltpu.VMEM((sup, LANES), F32),
            pltpu.VMEM((LANES, sup), F32),
            pltpu.VMEM((sup, width), F32),
            pltpu.VMEM((sup, width), F32),
            pltpu.VMEM((sup, width), F32),
            pltpu.VMEM((npair, 2 * sup, GDN_DK), BF16),
            pltpu.VMEM((npair, sup, GDN_DK), BF16),
            pltpu.VMEM((npair, sup, sup), F32),
            pltpu.VMEM((npair, sup, sup), BF16),
            pltpu.VMEM((npair, sup, sup), F32),
            pltpu.VMEM((npair, sup, sup), BF16),
            pltpu.VMEM((npair, sup, 2 * GDN_DV), BF16),
            pltpu.VMEM((2 * npair, sup, GDN_DK), BF16),
            pltpu.VMEM((npair, sup, GDN_DV), F32),
            pltpu.VMEM((npair, sup, GDN_DK), BF16),
            pltpu.VMEM((npair, sup, GDN_DV), BF16),
            pltpu.VMEM((npair, sup, GDN_DV), F32),
            pltpu.VMEM((2 * (rows_blk // sup), SUBLANES, width), F32),
            pltpu.VMEM((GDN_HEADS, GDN_DK, GDN_DV), F32),
        ],
        compiler_params=pltpu.CompilerParams(
            dimension_semantics=("parallel", "arbitrary"), vmem_limit_bytes=VMEM_LIMIT),
        name="gdn",
    )(proj, proj, proj, proj, small, cwq, cwk, cwv, dtb, alog, ng, eb, eg)


def _mix_body(x_ref, ys_ref, yg_ref, ms_ref, mg_ref, ws_ref, wg_ref, wo_ref, g_ref, o_ref):
    a = _dot(ys_ref[...], ws_ref[...])
    b = _dot(yg_ref[...], wg_ref[...])
    rows = x_ref.shape[0]
    ms = _lanes(ms_ref, 0, rows).astype(F32)
    mg = _lanes(mg_ref, 0, rows).astype(F32)
    mixed = jax.nn.sigmoid(ms) * a + jax.nn.sigmoid(mg) * b
    out = _dot(mixed.astype(BF16), wo_ref[...])
    o_ref[...] = x_ref[...] + _rms(out, g_ref[...])


def _mix(x, y_ssm, y_gdn, proj, w_s, w_g, w_o, g, tm=512):
    n = x.shape[0]
    rowblk = lambda col: pl.BlockSpec((tm, D_MODEL), lambda i: (i, col // D_MODEL))
    slabs = lambda col: pl.BlockSpec((D_MODEL // LANES, tm, LANES), lambda i: (col // D_MODEL, i, 0))
    full = lambda shape: pl.BlockSpec(shape, lambda i: (0,) * len(shape))
    return pl.pallas_call(
        _mix_body,
        grid=(n // tm,),
        in_specs=[rowblk(0), rowblk(0), rowblk(0), slabs(COL_MGS), slabs(COL_MGG),
                  full((D_MODEL, D_MODEL)), full((D_MODEL, D_MODEL)), full((D_MODEL, D_MODEL)),
                  full((1, D_MODEL))],
        out_specs=rowblk(0),
        out_shape=jax.ShapeDtypeStruct((n, D_MODEL), F32),
        compiler_params=pltpu.CompilerParams(
            dimension_semantics=("parallel",), vmem_limit_bytes=VMEM_LIMIT),
        name="mix",
    )(x, y_ssm, y_gdn, proj, proj, w_s, w_g, w_o, g)


def _ffn_body(x_ref, xp_ref, gpre_ref, wg_ref, wu_ref, wd_ref, cw_ref, cb_ref, gpost_ref, o_ref,
              h_ref, gbuf, acc_ref, *, tm, tf, tiles_per_seq):
    i = pl.program_id(0)
    j = pl.program_id(1)

    @pl.when(j == 0)
    def _():
        h_ref[FFN_HALO:FFN_HALO + tm, :] = _rms(x_ref[...], gpre_ref[...]).astype(BF16)
        h_ref[0:FFN_HALO, :] = _rms(xp_ref[...], gpre_ref[...]).astype(BF16)
        acc_ref[...] = jnp.zeros(acc_ref.shape, F32)

    seq_start = i % tiles_per_seq == 0

    def gate_up(s):
        cs = slice(s * MXU_COLS, (s + 1) * MXU_COLS)
        g = _dot(h_ref[...], wg_ref[:, cs])
        gbuf[FFN_HALO:FFN_HALO + tm, cs] = g[FFN_HALO:FFN_HALO + tm]
        gbuf[0:FFN_HALO, cs] = jnp.where(seq_start, 0.0, g[0:FFN_HALO])
        return _dot(h_ref[FFN_HALO:FFN_HALO + tm, :], wu_ref[:, cs])

    def act_down(s, up):
        cs = slice(s * MXU_COLS, (s + 1) * MXU_COLS)
        conv = _conv_rows(gbuf, cw_ref, tm, cs, FFN_CONV, FFN_HALO) + cb_ref[:, cs]
        acc_ref[...] += _dot((_gelu_tanh(conv) * up).astype(BF16), wd_ref[cs, :])

    ns = tf // MXU_COLS
    up = gate_up(0)
    for s in range(ns):
        up_next = gate_up(s + 1) if s + 1 < ns else None
        act_down(s, up)
        up = up_next

    @pl.when(j == pl.num_programs(1) - 1)
    def _():
        o_ref[...] = x_ref[...] + _rms(acc_ref[...], gpost_ref[...])


def _ffn(x, gpre, w_gate, w_up, w_down, cw, cb, gpost, seq, tm=1024, tf=512):
    n = x.shape[0]
    hb = tm // FFN_HALO
    return pl.pallas_call(
        functools.partial(_ffn_body, tm=tm, tf=tf, tiles_per_seq=seq // tm),
        grid=(n // tm, D_FF // tf),
        in_specs=[
            pl.BlockSpec((tm, D_MODEL), lambda i, j: (i, 0)),
            pl.BlockSpec((FFN_HALO, D_MODEL), lambda i, j: (jnp.maximum(i * hb - 1, 0), 0)),
            pl.BlockSpec((1, D_MODEL), lambda i, j: (0, 0)),
            pl.BlockSpec((D_MODEL, tf), lambda i, j: (0, j)),
            pl.BlockSpec((D_MODEL, tf), lambda i, j: (0, j)),
            pl.BlockSpec((tf, D_MODEL), lambda i, j: (j, 0)),
            pl.BlockSpec((FFN_CONV, tf), lambda i, j: (0, j)),
            pl.BlockSpec((1, tf), lambda i, j: (0, j)),
            pl.BlockSpec((1, D_MODEL), lambda i, j: (0, 0)),
        ],
        out_specs=pl.BlockSpec((tm, D_MODEL), lambda i, j: (i, 0)),
        out_shape=jax.ShapeDtypeStruct((n, D_MODEL), F32),
        scratch_shapes=[
            pltpu.VMEM((FFN_HALO + tm, D_MODEL), BF16),
            pltpu.VMEM((FFN_HALO + tm, tf), F32),
            pltpu.VMEM((tm, D_MODEL), F32),
        ],
        compiler_params=pltpu.CompilerParams(
            dimension_semantics=("parallel", "arbitrary"), vmem_limit_bytes=VMEM_LIMIT),
        name="ffn",
    )(x, x, gpre, w_gate, w_up, w_down, cw, cb, gpost)


def _ple_body(x_ref, p_ref, wg_ref, wp_ref, g_ref, o_ref):
    x = x_ref[...]
    e = jax.nn.sigmoid(_dot(x.astype(BF16), wg_ref[...])) * _dot(p_ref[...].astype(BF16), wp_ref[...])
    o_ref[...] = x + _rms(e, g_ref[...])


def _ple(x, p_all, layer, w_gate, w_proj, g, tm=512):
    n = x.shape[0]
    return pl.pallas_call(
        _ple_body,
        grid=(n // tm,),
        in_specs=[
            pl.BlockSpec((tm, D_MODEL), lambda i: (i, 0)),
            pl.BlockSpec((None, tm, PLE_DIM), lambda i: (layer, i, 0)),
            pl.BlockSpec((D_MODEL, D_MODEL), lambda i: (0, 0)),
            pl.BlockSpec((PLE_DIM, D_MODEL), lambda i: (0, 0)),
            pl.BlockSpec((1, D_MODEL), lambda i: (0, 0)),
        ],
        out_specs=pl.BlockSpec((tm, D_MODEL), lambda i: (i, 0)),
        out_shape=jax.ShapeDtypeStruct((n, D_MODEL), F32),
        compiler_params=pltpu.CompilerParams(
            dimension_semantics=("parallel",), vmem_limit_bytes=VMEM_LIMIT),
        name="ple",
    )(x, p_all, w_gate, w_proj, g)


def _pad_lanes(v, offset):
    return jnp.zeros((1, LANES), F32).at[0, offset:offset + v.shape[0]].set(v.astype(F32))


def _expansion(offset, heads, width):
    r = jnp.arange(LANES)[:, None]
    c = jnp.arange(heads * width)[None, :]
    return (r == offset + c // width).astype(BF16)


def _split_w_in(w):
    i0 = 0
    z = w[:, i0:i0 + 1024]; i0 += 1024
    xs = w[:, i0:i0 + 1024]; i0 += 1024
    bc = w[:, i0:i0 + 512]; i0 += 512
    dt = w[:, i0:i0 + 16]; i0 += 16
    qkv = w[:, i0:i0 + 3072]; i0 += 3072
    gate = w[:, i0:i0 + 1024]; i0 += 1024
    b = w[:, i0:i0 + 8]; i0 += 8
    a = w[:, i0:i0 + 8]; i0 += 8
    mg = w[:, i0:i0 + 2048]
    big = jnp.concatenate([qkv, xs, z, gate, mg, bc], axis=1).astype(BF16)
    small = jnp.concatenate([dt, b, a, jnp.zeros((w.shape[0], LANES - 32), w.dtype)], axis=1).astype(BF16)
    return big, small


PROJ_DTYPE = F32


def kernel(x, p, g_mix_pre, w_in, conv_ssm_w, conv_ssm_b, ssm_dt_bias, ssm_a_log, ssm_d, ssm_norm_g,
           conv_gdn_w, gdn_dt_bias, gdn_a_log, gdn_norm_g, w_br_ssm, w_br_gdn, w_out, g_mix_post,
           g_ffn_pre, w_ffn_gate, w_ffn_up, conv_ffn_w, conv_ffn_b, w_ffn_down, g_ffn_post,
           w_ple_gate, w_ple_proj, g_ple_post):
    batch, seq, d = x.shape
    depth = w_in.shape[0]
    n = batch * seq
    assert d == D_MODEL and seq % 1024 == 0
    xf = x.reshape(n, d)
    p_all = p.reshape(depth, n, PLE_DIM)
    row = lambda v: v.astype(F32).reshape(1, -1)

    e_ssd = _expansion(SM_DT, SSM_HEADS, SSM_HEAD_DIM)
    e_beta = _expansion(SM_B, GDN_HEADS, GDN_DK)
    e_dec = _expansion(SM_A, GDN_HEADS, GDN_DK)

    for i in range(depth):
        w_big, w_small = _split_w_in(w_in[i])
        proj, small = _inproj(xf, row(g_mix_pre[i]), w_big, w_small, PROJ_DTYPE)

        cw = conv_ssm_w[i].astype(F32)
        cb = conv_ssm_b[i].astype(F32)
        y_ssm = _ssd(
            proj, small, cw[:, :SSM_D_INNER], cw[:, SSM_D_INNER:], row(cb[:SSM_D_INNER]), row(cb[SSM_D_INNER:]),
            _pad_lanes(ssm_dt_bias[i], SM_DT), _pad_lanes(ssm_a_log[i], SM_DT),
            row(jnp.repeat(ssm_d[i], SSM_HEAD_DIM)), row(ssm_norm_g[i]), e_ssd, batch, seq)

        cg = conv_gdn_w[i].astype(F32)
        y_gdn = _gdn(
            proj, small, cg[:, 0:1024], cg[:, 1024:2048], cg[:, 2048:3072],
            _pad_lanes(gdn_dt_bias[i], SM_A), _pad_lanes(gdn_a_log[i], SM_A), row(gdn_norm_g[i]),
            e_beta, e_dec, batch, seq)

        xf = _mix(xf, y_ssm, y_gdn, proj, w_br_ssm[i].astype(BF16), w_br_gdn[i].astype(BF16),
                  w_out[i].astype(BF16), row(g_mix_post[i]))
        xf = _ffn(xf, row(g_ffn_pre[i]), w_ffn_gate[i].astype(BF16), w_ffn_up[i].astype(BF16),
                  w_ffn_down[i].astype(BF16), conv_ffn_w[i].astype(F32), row(conv_ffn_b[i]),
                  row(g_ffn_post[i]), seq)
        xf = _ple(xf, p_all, i, w_ple_gate[i].astype(BF16), w_ple_proj[i].astype(BF16), row(g_ple_post[i]))
    return xf.reshape(batch, seq, d)
```

```python
import functools

import jax
import jax.numpy as jnp
from jax import lax
from jax.experimental import pallas as pl
from jax.experimental.pallas import tpu as pltpu

F32 = jnp.float32
BF16 = jnp.bfloat16
EPS = 1e-6

D_MODEL = 1024
CHUNK = 64
SSM_HEADS = 16
SSM_HEAD_DIM = 64
SSM_D_INNER = 1024
SSM_GROUPS = 2
SSM_STATE = 128
SSM_CONV = 4
GDN_HEADS = 8
GDN_DK = 128
GDN_DV = 128
GDN_CONV = 4
D_FF = 4096
FFN_CONV = 3
PLE_DIM = 256

LANES = 128
SUBLANES = 8
MXU_COLS = 256
CONV_STRIDE = 4
FFN_HALO = 16
VMEM_LIMIT = 48 * 1024 * 1024
FFN_VMEM_LIMIT = 56 * 1024 * 1024

COL_Q, COL_K, COL_V, COL_XS, COL_Z, COL_GATE, COL_MGS, COL_MGG, COL_BC = (
    0, 1024, 2048, 3072, 4096, 5120, 6144, 7168, 8192)
PROJ_COLS = 8704
SM_DT, SM_B, SM_A = 0, 16, 24


def _rms(x, g):
    return x * lax.rsqrt(jnp.mean(x * x, axis=-1, keepdims=True) + EPS) * g


def _silu(x):
    return x * jax.nn.sigmoid(x)


def _softplus(x):
    return jnp.maximum(x, 0.0) + jnp.log1p(jnp.exp(-jnp.abs(x)))


def _gelu_tanh(x):
    c = 0.7978845608028654
    return 0.5 * x * (1.0 + jnp.tanh(c * (x + 0.044715 * (x * x * x))))


def _split2(v):
    hi = v.astype(BF16)
    lo = (v - hi.astype(F32)).astype(BF16)
    return hi, lo


def _split3(v):
    hi = v.astype(BF16)
    r = v - hi.astype(F32)
    mid = r.astype(BF16)
    lo = (r - mid.astype(F32)).astype(BF16)
    return hi, mid, lo


def _dot(a, b):
    return jnp.dot(a, b, preferred_element_type=F32)


def _dot_tb(a, b):
    return lax.dot_general(a, b, (((1,), (1,)), ((), ())), preferred_element_type=F32)


def _dot_ta(a, b):
    return lax.dot_general(a, b, (((0,), (0,)), ((), ())), preferred_element_type=F32)


def _expand(v, e):
    hi, lo = _split2(v)
    return _dot(hi, e) + _dot(lo, e)


def _conv_silu_tiles(x_ref, tail_ref, w_ref, b_ref, out_ref, r0, rows, taps):
    st = CONV_STRIDE
    assert taps - 1 <= st <= SUBLANES and rows % (SUBLANES * st) == 0 and r0 % (SUBLANES * st) == 0
    first_sublane = lax.broadcasted_iota(jnp.int32, (SUBLANES, LANES), 0) == 0
    for lt in range(x_ref.shape[0]):
        ln = slice(lt * LANES, (lt + 1) * LANES)
        w = [jnp.broadcast_to(w_ref[k:k + 1, ln], (SUBLANES, LANES)) for k in range(taps)]
        bias = None if b_ref is None else jnp.broadcast_to(b_ref[:, ln], (SUBLANES, LANES))
        for a in range(r0, r0 + rows, SUBLANES * st):
            v = [x_ref[lt, pl.ds(a + j, SUBLANES, stride=st), :] for j in range(st)]
            hist = []
            for d in range(1, taps):
                before = tail_ref[lt, SUBLANES - d:SUBLANES - d + 1, :] if a == 0 else x_ref[lt, a - d:a - d + 1, :]
                hist.append(jnp.where(first_sublane, before, pltpu.roll(v[st - d], 1, 0)))
            for j in range(st):
                acc = w[taps - 1] * v[j]
                for d in range(1, taps):
                    acc = acc + w[taps - 1 - d] * (v[j - d] if j >= d else hist[d - j - 1])
                if bias is not None:
                    acc = acc + bias
                out_ref[lt, pl.ds(a + j, SUBLANES, stride=st), :] = _silu(acc)


def _save_tail(x_ref, tail_ref):
    rows = x_ref.shape[1]
    for lt in range(x_ref.shape[0]):
        tail_ref[lt] = x_ref[lt, rows - SUBLANES:rows, :]


def _lanes(ref, r0, rows):
    return jnp.concatenate([ref[lt, r0:r0 + rows, :] for lt in range(ref.shape[0])], axis=1)


def _spread(main, extra):
    out = list(main)
    for k, item in enumerate(extra):
        out.insert((k + 1) * len(main) // (len(extra) + 1) + k, item)
    return out


def _emit(groups, side):
    done = 0
    for gi, grp in enumerate(groups):
        grp()
        upto = (gi + 1) * len(side) // len(groups)
        for item in side[done:upto]:
            item()
        done = upto


def _inproj_body(x_ref, g_ref, w_ref, ws_ref, o_ref, os_ref, h_ref):
    @pl.when(pl.program_id(1) == 0)
    def _():
        hb = _rms(x_ref[...], g_ref[...]).astype(BF16)
        h_ref[...] = hb
        os_ref[...] = _dot(hb, ws_ref[...])

    res = _dot(h_ref[...], w_ref[...]).astype(o_ref.dtype)
    for lt in range(o_ref.shape[0]):
        o_ref[lt] = res[:, lt * LANES:(lt + 1) * LANES]


def _inproj(x, g, w_big, w_small, proj_dtype, tm=1024, tn=2176):
    n = x.shape[0]
    return pl.pallas_call(
        _inproj_body,
        grid=(n // tm, PROJ_COLS // tn),
        in_specs=[
            pl.BlockSpec((tm, D_MODEL), lambda i, j: (i, 0)),
            pl.BlockSpec((1, D_MODEL), lambda i, j: (0, 0)),
            pl.BlockSpec((D_MODEL, tn), lambda i, j: (0, j)),
            pl.BlockSpec((D_MODEL, LANES), lambda i, j: (0, 0)),
        ],
        out_specs=[
            pl.BlockSpec((tn // LANES, tm, LANES), lambda i, j: (j, i, 0)),
            pl.BlockSpec((tm, LANES), lambda i, j: (i, 0)),
        ],
        out_shape=[
            jax.ShapeDtypeStruct((PROJ_COLS // LANES, n, LANES), proj_dtype),
            jax.ShapeDtypeStruct((n, LANES), F32),
        ],
        scratch_shapes=[pltpu.VMEM((tm, D_MODEL), BF16)],
        compiler_params=pltpu.CompilerParams(
            dimension_semantics=("parallel", "arbitrary"), vmem_limit_bytes=VMEM_LIMIT),
        name="inproj",
    )(x, g, w_big, w_small)


def _ssd_body(xs_ref, bc_ref, z_ref, sm_ref, cwx_ref, cwbc_ref, cbx_ref, cbbc_ref, dtb_ref, alog_ref,
              dsk_ref, ng_ref, e_ref, o_ref, xtail, bctail, xc, bcc, state, *, rows_blk):
    q = CHUNK
    gw = SSM_D_INNER // SSM_GROUPS

    @pl.when(pl.program_id(1) == 0)
    def _():
        xtail[...] = jnp.zeros(xtail.shape, F32)
        bctail[...] = jnp.zeros(bctail.shape, F32)
        state[...] = jnp.zeros(state.shape, F32)

    _conv_silu_tiles(xs_ref, xtail, cwx_ref, cbx_ref, xc, 0, rows_blk, SSM_CONV)
    _conv_silu_tiles(bc_ref, bctail, cwbc_ref, cbbc_ref, bcc, 0, rows_blk, SSM_CONV)
    _save_tail(xs_ref, xtail)
    _save_tail(bc_ref, bctail)

    lane = lax.broadcasted_iota(jnp.int32, (q, LANES), 1)
    rowi = lax.broadcasted_iota(jnp.int32, (q, LANES), 0)
    lo_half = lane < q
    causal2 = rowi >= (lane % q)
    li = lax.broadcasted_iota(jnp.int32, (q, q), 0)
    lj = lax.broadcasted_iota(jnp.int32, (q, q), 1)
    l_incl = (lj <= li).astype(BF16)
    u2 = (rowi <= (lane % q)).astype(BF16)
    r2 = lax.broadcasted_iota(jnp.int32, (2 * q, LANES), 0)
    c2 = lax.broadcasted_iota(jnp.int32, (2 * q, LANES), 1)
    blockdiag = (r2 < q) == (c2 < q)
    dt_lane = lax.broadcasted_iota(jnp.int32, (q, LANES), 1) < SSM_HEADS

    a_vec = -jnp.exp(alog_ref[...])
    e_mat = e_ref[...]

    for c in range(rows_blk // q):
        r0 = c * q
        xcv = _lanes(xc, r0, q)
        dt = jnp.where(dt_lane, _softplus(sm_ref[r0:r0 + q, :] + dtb_ref[...]), 0.0)
        adt = dt * a_vec
        parts = _split3(adt)
        acum = _dot(l_incl, parts[0]) + _dot(l_incl, parts[1]) + _dot(l_incl, parts[2])
        acum_t = _dot_ta(parts[0], u2) + _dot_ta(parts[1], u2) + _dot_ta(parts[2], u2)
        ea = jnp.exp(acum)
        ds = jnp.exp(acum[q - 1:q, :] - acum)
        ex = _expand(jnp.concatenate([dt, ea, ds], axis=0), e_mat)
        dt_e, ea_e, ds_e = ex[0:q], ex[q:2 * q], ex[2 * q:3 * q]
        xdt = xcv * dt_e
        xds = (xdt * ds_e).astype(BF16)
        y_parts = []
        for g in range(SSM_GROUPS):
            bg = bcc[g, r0:r0 + q, :].astype(BF16)
            cg = bcc[SSM_GROUPS + g, r0:r0 + q, :].astype(BF16)
            sc2 = _dot_tb(cg, jnp.concatenate([bg, bg], axis=0))
            st = state[:, g * gw:(g + 1) * gw]
            yoff = _dot(cg, st.astype(BF16)) * ea_e[:, g * gw:(g + 1) * gw]
            for pr in range(gw // LANES):
                h0 = g * (SSM_HEADS // SSM_GROUPS) + 2 * pr
                l0 = g * gw + pr * LANES
                col = jnp.where(lo_half, acum[:, h0:h0 + 1], acum[:, h0 + 1:h0 + 2])
                row = jnp.where(lo_half, acum_t[h0:h0 + 1, :], acum_t[h0 + 1:h0 + 2, :])
                lm = jnp.exp(jnp.where(causal2, col - row, -jnp.inf))
                pm = (sc2 * lm).astype(BF16)
                xp = xdt[:, l0:l0 + LANES]
                rhs = jnp.where(blockdiag, jnp.concatenate([xp, xp], axis=0), 0.0).astype(BF16)
                y_parts.append(_dot(pm, rhs) + yoff[:, pr * LANES:(pr + 1) * LANES])
            contrib = _dot_ta(bg, xds[:, g * gw:(g + 1) * gw])
            state[:, g * gw:(g + 1) * gw] = st * ea_e[q - 1:q, g * gw:(g + 1) * gw] + contrib
        y = jnp.concatenate(y_parts, axis=1) + dsk_ref[...] * xcv
        yz = y * _silu(_lanes(z_ref, r0, q).astype(F32))
        outs = []
        for g in range(SSM_GROUPS):
            part = yz[:, g * gw:(g + 1) * gw]
            outs.append(part * lax.rsqrt(jnp.mean(part * part, axis=-1, keepdims=True) + EPS))
        o_ref[r0:r0 + q, :] = (jnp.concatenate(outs, axis=1) * ng_ref[...]).astype(o_ref.dtype)


def _ssd(proj, small, cwx, cwbc, cbx, cbbc, dtb, alog, dsk, ng, e_mat, batch, seq, rows_blk=256):
    nt = seq // rows_blk
    n = batch * seq
    bcw = 2 * SSM_GROUPS * SSM_STATE
    xt = SSM_D_INNER // LANES
    bt = bcw // LANES
    row = lambda b, t: b * nt + t
    full = lambda shape: pl.BlockSpec(shape, lambda b, t: (0,) * len(shape))
    return pl.pallas_call(
        functools.partial(_ssd_body, rows_blk=rows_blk),
        grid=(batch, nt),
        in_specs=[
            pl.BlockSpec((xt, rows_blk, LANES), lambda b, t: (COL_XS // SSM_D_INNER, row(b, t), 0)),
            pl.BlockSpec((bt, rows_blk, LANES), lambda b, t: (COL_BC // bcw, row(b, t), 0)),
            pl.BlockSpec((xt, rows_blk, LANES), lambda b, t: (COL_Z // SSM_D_INNER, row(b, t), 0)),
            pl.BlockSpec((rows_blk, LANES), lambda b, t: (row(b, t), 0)),
            full((SSM_CONV, SSM_D_INNER)), full((SSM_CONV, bcw)),
            full((1, SSM_D_INNER)), full((1, bcw)),
            full((1, LANES)), full((1, LANES)),
            full((1, SSM_D_INNER)), full((1, SSM_D_INNER)),
            full((LANES, SSM_D_INNER)),
        ],
        out_specs=pl.BlockSpec((rows_blk, SSM_D_INNER), lambda b, t: (row(b, t), 0)),
        out_shape=jax.ShapeDtypeStruct((n, SSM_D_INNER), BF16),
        scratch_shapes=[
            pltpu.VMEM((xt, SUBLANES, LANES), F32),
            pltpu.VMEM((bt, SUBLANES, LANES), F32),
            pltpu.VMEM((xt, rows_blk, LANES), F32),
            pltpu.VMEM((bt, rows_blk, LANES), F32),
            pltpu.VMEM((SSM_STATE, SSM_D_INNER), F32),
        ],
        compiler_params=pltpu.CompilerParams(
            dimension_semantics=("parallel", "arbitrary"), vmem_limit_bytes=VMEM_LIMIT),
        name="ssd",
    )(proj, proj, proj, small, cwx, cwbc, cbx, cbbc, dtb, alog, dsk, ng, e_mat)


def _gdn_body(q_ref, k_ref, v_ref, gate_ref, sm_ref, cwq_ref, cwk_ref, cwv_ref, dtb_ref, alog_ref, ng_ref,
              o_ref,
              qtail, ktail, vtail, qc, kc, vc, gc_ref, gct_ref, beta_ref, eg_ref, ekd_ref,
              kbq_ref, kn_ref, dec_ref, m_ref, p_ref, qk_ref, rhs_ref, wq_ref, u_ref, kd_ref, vn_ref, os_ref,
              last_ref, s_ref,
              *, rows_blk):
    sup = 2 * CHUNK
    nsc = rows_blk // sup
    nh = GDN_HEADS

    @pl.when(pl.program_id(1) == 0)
    def _():
        qtail[...] = jnp.zeros(qtail.shape, F32)
        ktail[...] = jnp.zeros(ktail.shape, F32)
        vtail[...] = jnp.zeros(vtail.shape, F32)
        s_ref[...] = jnp.zeros(s_ref.shape, F32)

    def conv(sc, stream):
        src, tail, cw, dst = ((q_ref, qtail, cwq_ref, qc), (k_ref, ktail, cwk_ref, kc),
                              (v_ref, vtail, cwv_ref, vc))[stream]
        _conv_silu_tiles(src, tail, cw, None, dst, sc * sup, sup, GDN_CONV)

    ri = lax.broadcasted_iota(jnp.int32, (sup, sup), 0)
    ci = lax.broadcasted_iota(jnp.int32, (sup, sup), 1)
    same = (ri < CHUNK) == (ci < CHUNK)
    incl = same & (ci <= ri)
    strict = same & (ci < ri)
    l_bd = incl.astype(BF16)
    u_bd = (same & (ri <= ci)).astype(BF16)
    eye = (ri == ci).astype(F32)
    lane = lax.broadcasted_iota(jnp.int32, (sup, LANES), 1)
    g_lane = (lane >= SM_A) & (lane < SM_A + GDN_HEADS)
    first_chunk = lax.broadcasted_iota(jnp.int32, (sup, LANES), 0) < CHUNK

    a_vec = -jnp.exp(alog_ref[...])
    scale = GDN_DK ** -0.5

    def pair(sc, h):
        return sc * nh + h

    def prep(sc):
        r0 = sc * sup
        sm = sm_ref[r0:r0 + sup, :]
        gl = jnp.where(g_lane, a_vec * _softplus(sm + dtb_ref[...]), 0.0)
        parts = _split3(gl)
        gc = _dot(l_bd, parts[0]) + _dot(l_bd, parts[1]) + _dot(l_bd, parts[2])
        gc_ref[...] = gc
        gct_ref[...] = _dot_ta(parts[0], u_bd) + _dot_ta(parts[1], u_bd) + _dot_ta(parts[2], u_bd)
        eg = jnp.exp(gc)
        gc_last = jnp.where(first_chunk, gc[CHUNK - 1:CHUNK, :], gc[sup - 1:sup, :])
        beta_ref[...] = jax.nn.sigmoid(sm)
        eg_ref[...] = eg
        ekd_ref[...] = jnp.exp(gc_last - gc)
        for c in range(2):
            last_ref[2 * sc + c] = jnp.broadcast_to(eg[(c + 1) * CHUNK - 1:(c + 1) * CHUNK, :], (SUBLANES, LANES))

    def a_vector(sc, h):
        p = pair(sc, h)
        r0 = sc * sup
        ln = slice(h * GDN_DK, (h + 1) * GDN_DK)
        qh = qc[h, r0:r0 + sup, :]
        kh = kc[h, r0:r0 + sup, :]
        vh = vc[h, r0:r0 + sup, :]
        qs = qh * (lax.rsqrt(jnp.sum(qh * qh, axis=-1, keepdims=True) + EPS) * scale)
        kn = kh * lax.rsqrt(jnp.sum(kh * kh, axis=-1, keepdims=True) + EPS)
        b_h = jnp.broadcast_to(beta_ref[:, SM_B + h:SM_B + h + 1], (sup, LANES))
        eg_h = jnp.broadcast_to(eg_ref[:, SM_A + h:SM_A + h + 1], (sup, LANES))
        ekd_h = jnp.broadcast_to(ekd_ref[:, SM_A + h:SM_A + h + 1], (sup, LANES))
        kb = kn * b_h
        kbq_ref[p] = jnp.concatenate([kb, qs], axis=0).astype(BF16)
        kn_ref[p] = kn.astype(BF16)
        col = gc_ref[:, SM_A + h:SM_A + h + 1]
        row = gct_ref[SM_A + h:SM_A + h + 1, :]
        dec_ref[p] = jnp.exp(jnp.where(incl, col - row, -jnp.inf))
        rhs_ref[p] = jnp.concatenate([vh * b_h, kb * eg_h], axis=1).astype(BF16)
        qd = (qs * eg_h).astype(BF16)
        for c in range(2):
            wq_ref[2 * p + c, CHUNK:2 * CHUNK, :] = qd[c * CHUNK:(c + 1) * CHUNK]
        kd_ref[p] = (kn * ekd_h).astype(BF16)

    def a_matmul(sc):
        for h in range(nh):
            p = pair(sc, h)
            kq = _dot_tb(kbq_ref[p], kn_ref[p])
            dec = dec_ref[p]
            nmat = jnp.where(strict, -(kq[0:sup] * dec), 0.0)
            m_ref[p] = nmat.astype(BF16)
            p_ref[p] = eye + nmat
            qk_ref[p] = (kq[sup:2 * sup] * dec).astype(BF16)

    def square(p):
        mb = m_ref[p]
        m_ref[p] = _dot(mb, mb).astype(BF16)

    def accumulate(p):
        pv = p_ref[p]
        p_ref[p] = pv + _dot(pv.astype(BF16), m_ref[p])

    def phase_b_groups(sc):
        groups = []
        for _ in range(5):
            groups.append(lambda: [square(pair(sc, h)) for h in range(nh)])
            groups.append(lambda: [accumulate(pair(sc, h)) for h in range(nh)])
        return groups

    def phase_c(sc):
        for h in range(nh):
            p = pair(sc, h)
            uw = _dot(p_ref[p].astype(BF16), rhs_ref[p])
            u_ref[p] = uw[:, 0:GDN_DV]
            w = uw[:, GDN_DV:2 * GDN_DV].astype(BF16)
            for c in range(2):
                wq_ref[2 * p + c, 0:CHUNK, :] = w[c * CHUNK:(c + 1) * CHUNK]

    def d_apply(sc, c):
        rows = slice(c * CHUNK, (c + 1) * CHUNK)
        for h in range(nh):
            p = pair(sc, h)
            r = _dot(wq_ref[2 * p + c], s_ref[h].astype(BF16))
            vn_ref[p, rows, :] = (u_ref[p, rows, :] - r[0:CHUNK]).astype(BF16)
            os_ref[p, rows, :] = r[CHUNK:2 * CHUNK]

    def d_update(sc, c):
        rows = slice(c * CHUNK, (c + 1) * CHUNK)
        for h in range(nh):
            p = pair(sc, h)
            last = last_ref[2 * sc + c, 0:1, SM_A + h:SM_A + h + 1]
            s_ref[h] = s_ref[h] * last + _dot_ta(kd_ref[p, rows, :], vn_ref[p, rows, :])

    def phase_d_items(sc):
        items = []
        for c in range(2):
            items.append(functools.partial(d_apply, sc, c))
            items.append(functools.partial(d_update, sc, c))
        return items

    def phase_e(sc, h):
        p = pair(sc, h)
        r0 = sc * sup
        ln = slice(h * GDN_DK, (h + 1) * GDN_DK)
        o = os_ref[p] + _dot(qk_ref[p], vn_ref[p])
        o = o * lax.rsqrt(jnp.mean(o * o, axis=-1, keepdims=True) + EPS) * ng_ref[...]
        o = o * _silu(gate_ref[h, r0:r0 + sup, :].astype(F32))
        o_ref[r0:r0 + sup, ln] = o.astype(o_ref.dtype)

    def phase_e_items(sc):
        return [functools.partial(phase_e, sc, h) for h in range(nh)]

    def a_vector_items(sc):
        return ([functools.partial(prep, sc)] + [functools.partial(conv, sc, s) for s in range(3)]
                + [functools.partial(a_vector, sc, h) for h in range(nh)])

    for item in a_vector_items(0):
        item()
    a_matmul(0)
    for sc in range(nsc):
        side = _spread(a_vector_items(sc + 1) if sc + 1 < nsc else [],
                       phase_d_items(sc - 1) if sc >= 1 else [])
        _emit(phase_b_groups(sc), side)
        phase_c(sc)
        if sc + 1 < nsc:
            a_matmul(sc + 1)
        if sc >= 1:
            for item in phase_e_items(sc - 1):
                item()
    for item in phase_d_items(nsc - 1) + phase_e_items(nsc - 1):
        item()
    _save_tail(q_ref, qtail)
    _save_tail(k_ref, ktail)
    _save_tail(v_ref, vtail)


def _gdn(proj, small, cwq, cwk, cwv, dtb, alog, ng, batch, seq, rows_blk=512):
    nt = seq // rows_blk
    n = batch * seq
    width = GDN_HEADS * GDN_DK
    sup = 2 * CHUNK
    npair = (rows_blk // sup) * GDN_HEADS
    row = lambda b, t: b * nt + t
    full = lambda shape: pl.BlockSpec(shape, lambda b, t: (0,) * len(shape))
    colblk = lambda col: pl.BlockSpec((GDN_HEADS, rows_blk, LANES), lambda b, t: (col // width, row(b, t), 0))
    return pl.pallas_call(
        functools.partial(_gdn_body, rows_blk=rows_blk),
        grid=(batch, nt),
        in_specs=[
            colblk(COL_Q), colblk(COL_K), colblk(COL_V), colblk(COL_GATE),
            pl.BlockSpec((rows_blk, LANES), lambda b, t: (row(b, t), 0)),
            full((GDN_CONV, width)), full((GDN_CONV, width)), full((GDN_CONV, width)),
            full((1, LANES)), full((1, LANES)), full((1, GDN_DV)),
        ],
        out_specs=pl.BlockSpec((rows_blk, width), lambda b, t: (row(b, t), 0)),
        out_shape=jax.ShapeDtypeStruct((n, width), BF16),
        scratch_shapes=[
            pltpu.VMEM((GDN_HEADS, SUBLANES, LANES), F32),
            pltpu.VMEM((GDN_HEADS, SUBLANES, LANES), F32),
            pltpu.VMEM((GDN_HEADS, SUBLANES, LANES), F32),
            pltpu.VMEM((GDN_HEADS, rows_blk, LANES), F32),
            pltpu.VMEM((GDN_HEADS, rows_blk, LANES), F32),
            pltpu.VMEM((GDN_HEADS, rows_blk, LANES), F32),
            pltpu.VMEM((sup, LANES), F32),
            pltpu.VMEM((LANES, sup), F32),
            pltpu.VMEM((sup, LANES), F32),
            pltpu.VMEM((sup, LANES), F32),
            pltpu.VMEM((sup, LANES), F32),
            pltpu.VMEM((npair, 2 * sup, GDN_DK), BF16),
            pltpu.VMEM((npair, sup, GDN_DK), BF16),
            pltpu.VMEM((npair, sup, sup), F32),
            pltpu.VMEM((npair, sup, sup), BF16),
            pltpu.VMEM((npair, sup, sup), F32),
            pltpu.VMEM((npair, sup, sup), BF16),
            pltpu.VMEM((npair, sup, 2 * GDN_DV), BF16),
            pltpu.VMEM((2 * npair, sup, GDN_DK), BF16),
            pltpu.VMEM((npair, sup, GDN_DV), F32),
            pltpu.VMEM((npair, sup, GDN_DK), BF16),
            pltpu.VMEM((npair, sup, GDN_DV), BF16),
            pltpu.VMEM((npair, sup, GDN_DV), F32),
            pltpu.VMEM((2 * (rows_blk // sup), SUBLANES, LANES), F32),
            pltpu.VMEM((GDN_HEADS, GDN_DK, GDN_DV), F32),
        ],
        compiler_params=pltpu.CompilerParams(
            dimension_semantics=("parallel", "arbitrary"), vmem_limit_bytes=VMEM_LIMIT),
        name="gdn",
    )(proj, proj, proj, proj, small, cwq, cwk, cwv, dtb, alog, ng)


def _mix_body(x_ref, ys_ref, yg_ref, ms_ref, mg_ref, ws_ref, wg_ref, wo_ref, g_ref, o_ref):
    a = _dot(ys_ref[...], ws_ref[...])
    b = _dot(yg_ref[...], wg_ref[...])
    rows = x_ref.shape[0]
    ms = _lanes(ms_ref, 0, rows).astype(F32)
    mg = _lanes(mg_ref, 0, rows).astype(F32)
    mixed = jax.nn.sigmoid(ms) * a + jax.nn.sigmoid(mg) * b
    out = _dot(mixed.astype(BF16), wo_ref[...])
    o_ref[...] = x_ref[...] + _rms(out, g_ref[...])


def _mix(x, y_ssm, y_gdn, proj, w_s, w_g, w_o, g, tm=512):
    n = x.shape[0]
    rowblk = lambda col: pl.BlockSpec((tm, D_MODEL), lambda i: (i, col // D_MODEL))
    slabs = lambda col: pl.BlockSpec((D_MODEL // LANES, tm, LANES), lambda i: (col // D_MODEL, i, 0))
    full = lambda shape: pl.BlockSpec(shape, lambda i: (0,) * len(shape))
    return pl.pallas_call(
        _mix_body,
        grid=(n // tm,),
        in_specs=[rowblk(0), rowblk(0), rowblk(0), slabs(COL_MGS), slabs(COL_MGG),
                  full((D_MODEL, D_MODEL)), full((D_MODEL, D_MODEL)), full((D_MODEL, D_MODEL)),
                  full((1, D_MODEL))],
        out_specs=rowblk(0),
        out_shape=jax.ShapeDtypeStruct((n, D_MODEL), F32),
        compiler_params=pltpu.CompilerParams(
            dimension_semantics=("parallel",), vmem_limit_bytes=VMEM_LIMIT),
        name="mix",
    )(x, y_ssm, y_gdn, proj, proj, w_s, w_g, w_o, g)


def _ffn_body(x_ref, xp_ref, gpre_ref, wg_ref, wu_ref, wd_ref, cw_ref, cb_ref, gpost_ref, o_ref,
              h_ref, gbuf, ubuf, abuf, acc_ref, *, tm, tf, tiles_per_seq):
    i = pl.program_id(0)
    j = pl.program_id(1)
    st = CONV_STRIDE
    tiles = MXU_COLS // LANES

    @pl.when(j == 0)
    def _():
        h_ref[FFN_HALO:FFN_HALO + tm, :] = _rms(x_ref[...], gpre_ref[...]).astype(BF16)
        h_ref[0:FFN_HALO, :] = _rms(xp_ref[...], gpre_ref[...]).astype(BF16)
        acc_ref[...] = jnp.zeros(acc_ref.shape, F32)

    seq_start = i % tiles_per_seq == 0
    first_sublane = lax.broadcasted_iota(jnp.int32, (SUBLANES, LANES), 0) == 0

    def gate_up(s):
        cs = slice(s * MXU_COLS, (s + 1) * MXU_COLS)
        g = _dot(h_ref[...], wg_ref[:, cs])
        u = _dot(h_ref[FFN_HALO:FFN_HALO + tm, :], wu_ref[:, cs])
        for t in range(tiles):
            ln = slice(t * LANES, (t + 1) * LANES)
            gbuf[s % 2, t, FFN_HALO:FFN_HALO + tm, :] = g[FFN_HALO:FFN_HALO + tm, ln]
            gbuf[s % 2, t, 0:FFN_HALO, :] = jnp.where(seq_start, 0.0, g[0:FFN_HALO, ln])
            ubuf[s % 2, t] = u[:, ln]

    def act_down(s):
        for t in range(tiles):
            ln = slice(s * MXU_COLS + t * LANES, s * MXU_COLS + (t + 1) * LANES)
            w = [jnp.broadcast_to(cw_ref[k:k + 1, ln], (SUBLANES, LANES)) for k in range(FFN_CONV)]
            bias = jnp.broadcast_to(cb_ref[:, ln], (SUBLANES, LANES))
            for a in range(0, tm, SUBLANES * st):
                base = FFN_HALO + a
                v = [gbuf[s % 2, t, pl.ds(base + r, SUBLANES, stride=st), :] for r in range(st)]
                hist = [jnp.where(first_sublane, gbuf[s % 2, t, base - d:base - d + 1, :], pltpu.roll(v[st - d], 1, 0))
                        for d in range(1, FFN_CONV)]
                for r in range(st):
                    acc = w[FFN_CONV - 1] * v[r] + bias
                    for d in range(1, FFN_CONV):
                        acc = acc + w[FFN_CONV - 1 - d] * (v[r - d] if r >= d else hist[d - r - 1])
                    up = ubuf[s % 2, t, pl.ds(a + r, SUBLANES, stride=st), :]
                    abuf[t, pl.ds(a + r, SUBLANES, stride=st), :] = _gelu_tanh(acc) * up
        cs = slice(s * MXU_COLS, (s + 1) * MXU_COLS)
        acc_ref[...] += _dot(_lanes(abuf, 0, tm).astype(BF16), wd_ref[cs, :])

    ns = tf // MXU_COLS
    gate_up(0)
    for s in range(ns):
        if s + 1 < ns:
            gate_up(s + 1)
        act_down(s)

    @pl.when(j == pl.num_programs(1) - 1)
    def _():
        o_ref[...] = x_ref[...] + _rms(acc_ref[...], gpost_ref[...])


def _ffn(x, gpre, w_gate, w_up, w_down, cw, cb, gpost, seq, tm=1024, tf=1024):
    n = x.shape[0]
    hb = tm // FFN_HALO
    return pl.pallas_call(
        functools.partial(_ffn_body, tm=tm, tf=tf, tiles_per_seq=seq // tm),
        grid=(n // tm, D_FF // tf),
        in_specs=[
            pl.BlockSpec((tm, D_MODEL), lambda i, j: (i, 0)),
            pl.BlockSpec((FFN_HALO, D_MODEL), lambda i, j: (jnp.maximum(i * hb - 1, 0), 0)),
            pl.BlockSpec((1, D_MODEL), lambda i, j: (0, 0)),
            pl.BlockSpec((D_MODEL, tf), lambda i, j: (0, j)),
            pl.BlockSpec((D_MODEL, tf), lambda i, j: (0, j)),
            pl.BlockSpec((tf, D_MODEL), lambda i, j: (j, 0)),
            pl.BlockSpec((FFN_CONV, tf), lambda i, j: (0, j)),
            pl.BlockSpec((1, tf), lambda i, j: (0, j)),
            pl.BlockSpec((1, D_MODEL), lambda i, j: (0, 0)),
        ],
        out_specs=pl.BlockSpec((tm, D_MODEL), lambda i, j: (i, 0)),
        out_shape=jax.ShapeDtypeStruct((n, D_MODEL), F32),
        scratch_shapes=[
            pltpu.VMEM((FFN_HALO + tm, D_MODEL), BF16),
            pltpu.VMEM((2, MXU_COLS // LANES, FFN_HALO + tm, LANES), F32),
            pltpu.VMEM((2, MXU_COLS // LANES, tm, LANES), F32),
            pltpu.VMEM((MXU_COLS // LANES, tm, LANES), F32),
            pltpu.VMEM((tm, D_MODEL), F32),
        ],
        compiler_params=pltpu.CompilerParams(
            dimension_semantics=("parallel", "arbitrary"), vmem_limit_bytes=FFN_VMEM_LIMIT),
        name="ffn",
    )(x, x, gpre, w_gate, w_up, w_down, cw, cb, gpost)


def _ple_body(x_ref, p_ref, wg_ref, wp_ref, g_ref, o_ref):
    x = x_ref[...]
    e = jax.nn.sigmoid(_dot(x.astype(BF16), wg_ref[...])) * _dot(p_ref[...].astype(BF16), wp_ref[...])
    o_ref[...] = x + _rms(e, g_ref[...])


def _ple(x, p_all, layer, w_gate, w_proj, g, tm=1024):
    n = x.shape[0]
    return pl.pallas_call(
        _ple_body,
        grid=(n // tm,),
        in_specs=[
            pl.BlockSpec((tm, D_MODEL), lambda i: (i, 0)),
            pl.BlockSpec((None, tm, PLE_DIM), lambda i: (layer, i, 0)),
            pl.BlockSpec((D_MODEL, D_MODEL), lambda i: (0, 0)),
            pl.BlockSpec((PLE_DIM, D_MODEL), lambda i: (0, 0)),
            pl.BlockSpec((1, D_MODEL), lambda i: (0, 0)),
        ],
        out_specs=pl.BlockSpec((tm, D_MODEL), lambda i: (i, 0)),
        out_shape=jax.ShapeDtypeStruct((n, D_MODEL), F32),
        compiler_params=pltpu.CompilerParams(
            dimension_semantics=("parallel",), vmem_limit_bytes=VMEM_LIMIT),
        name="ple",
    )(x, p_all, w_gate, w_proj, g)


def _pad_lanes(v, offset):
    return jnp.zeros((1, LANES), F32).at[0, offset:offset + v.shape[0]].set(v.astype(F32))


def _expansion(offset, heads, width):
    r = jnp.arange(LANES)[:, None]
    c = jnp.arange(heads * width)[None, :]
    return (r == offset + c // width).astype(BF16)


def _split_w_in(w):
    i0 = 0
    z = w[:, i0:i0 + 1024]; i0 += 1024
    xs = w[:, i0:i0 + 1024]; i0 += 1024
    bc = w[:, i0:i0 + 512]; i0 += 512
    dt = w[:, i0:i0 + 16]; i0 += 16
    qkv = w[:, i0:i0 + 3072]; i0 += 3072
    gate = w[:, i0:i0 + 1024]; i0 += 1024
    b = w[:, i0:i0 + 8]; i0 += 8
    a = w[:, i0:i0 + 8]; i0 += 8
    mg = w[:, i0:i0 + 2048]
    big = jnp.concatenate([qkv, xs, z, gate, mg, bc], axis=1).astype(BF16)
    small = jnp.concatenate([dt, b, a, jnp.zeros((w.shape[0], LANES - 32), w.dtype)], axis=1).astype(BF16)
    return big, small


PROJ_DTYPE = F32


def kernel(x, p, g_mix_pre, w_in, conv_ssm_w, conv_ssm_b, ssm_dt_bias, ssm_a_log, ssm_d, ssm_norm_g,
           conv_gdn_w, gdn_dt_bias, gdn_a_log, gdn_norm_g, w_br_ssm, w_br_gdn, w_out, g_mix_post,
           g_ffn_pre, w_ffn_gate, w_ffn_up, conv_ffn_w, conv_ffn_b, w_ffn_down, g_ffn_post,
           w_ple_gate, w_ple_proj, g_ple_post):
    batch, seq, d = x.shape
    depth = w_in.shape[0]
    n = batch * seq
    assert d == D_MODEL and seq % 1024 == 0
    xf = x.reshape(n, d)
    p_all = p.reshape(depth, n, PLE_DIM)
    row = lambda v: v.astype(F32).reshape(1, -1)

    e_ssd = _expansion(SM_DT, SSM_HEADS, SSM_HEAD_DIM)

    for i in range(depth):
        w_big, w_small = _split_w_in(w_in[i])
        proj, small = _inproj(xf, row(g_mix_pre[i]), w_big, w_small, PROJ_DTYPE)

        cw = conv_ssm_w[i].astype(F32)
        cb = conv_ssm_b[i].astype(F32)
        y_ssm = _ssd(
            proj, small, cw[:, :SSM_D_INNER], cw[:, SSM_D_INNER:], row(cb[:SSM_D_INNER]), row(cb[SSM_D_INNER:]),
            _pad_lanes(ssm_dt_bias[i], SM_DT), _pad_lanes(ssm_a_log[i], SM_DT),
            row(jnp.repeat(ssm_d[i], SSM_HEAD_DIM)), row(ssm_norm_g[i]), e_ssd, batch, seq)

        cg = conv_gdn_w[i].astype(F32)
        y_gdn = _gdn(
            proj, small, cg[:, 0:1024], cg[:, 1024:2048], cg[:, 2048:3072],
            _pad_lanes(gdn_dt_bias[i], SM_A), _pad_lanes(gdn_a_log[i], SM_A), row(gdn_norm_g[i]),
            batch, seq)

        xf = _mix(xf, y_ssm, y_gdn, proj, w_br_ssm[i].astype(BF16), w_br_gdn[i].astype(BF16),
                  w_out[i].astype(BF16), row(g_mix_post[i]))
        xf = _ffn(xf, row(g_ffn_pre[i]), w_ffn_gate[i].astype(BF16), w_ffn_up[i].astype(BF16),
                  w_ffn_down[i].astype(BF16), conv_ffn_w[i].astype(F32), row(conv_ffn_b[i]),
                  row(g_ffn_post[i]), seq)
        xf = _ple(xf, p_all, i, w_ple_gate[i].astype(BF16), w_ple_proj[i].astype(BF16), row(g_ple_post[i]))
    return xf.reshape(batch, seq, d)
```

```python
import functools

import jax
import jax.numpy as jnp
from jax import lax
from jax.experimental import pallas as pl
from jax.experimental.pallas import tpu as pltpu

F32 = jnp.float32
BF16 = jnp.bfloat16
EPS = 1e-6

D_MODEL = 1024
CHUNK = 64
SSM_HEADS = 16
SSM_HEAD_DIM = 64
SSM_D_INNER = 1024
SSM_GROUPS = 2
SSM_STATE = 128
SSM_CONV = 4
GDN_HEADS = 8
GDN_DK = 128
GDN_DV = 128
GDN_CONV = 4
D_FF = 4096
FFN_CONV = 3
PLE_DIM = 256

LANES = 128
SUBLANES = 8
MXU_COLS = 256
CONV_STRIDE = 4
FFN_HALO = 16
VMEM_LIMIT = 48 * 1024 * 1024

COL_Q, COL_K, COL_V, COL_XS, COL_Z, COL_GATE, COL_MGS, COL_MGG, COL_BC = (
    0, 1024, 2048, 3072, 4096, 5120, 6144, 7168, 8192)
PROJ_COLS = 8704
SM_DT, SM_B, SM_A = 0, 16, 24


def _rms(x, g):
    return x * lax.rsqrt(jnp.mean(x * x, axis=-1, keepdims=True) + EPS) * g


def _silu(x):
    return x * jax.nn.sigmoid(x)


def _softplus(x):
    return jnp.maximum(x, 0.0) + jnp.log1p(jnp.exp(-jnp.abs(x)))


def _gelu_tanh(x):
    c = 0.7978845608028654
    return 0.5 * x * (1.0 + jnp.tanh(c * (x + 0.044715 * (x * x * x))))


def _split2(v):
    hi = v.astype(BF16)
    lo = (v - hi.astype(F32)).astype(BF16)
    return hi, lo


def _split3(v):
    hi = v.astype(BF16)
    r = v - hi.astype(F32)
    mid = r.astype(BF16)
    lo = (r - mid.astype(F32)).astype(BF16)
    return hi, mid, lo


def _dot(a, b):
    return jnp.dot(a, b, preferred_element_type=F32)


def _dot_tb(a, b):
    return lax.dot_general(a, b, (((1,), (1,)), ((), ())), preferred_element_type=F32)


def _dot_ta(a, b):
    return lax.dot_general(a, b, (((0,), (0,)), ((), ())), preferred_element_type=F32)


def _expand(v, e):
    hi, lo = _split2(v)
    return _dot(hi, e) + _dot(lo, e)


def _conv_silu_tiles(x_ref, tail_ref, w_ref, b_ref, out_ref, r0, rows, taps):
    st = CONV_STRIDE
    assert taps - 1 <= st <= SUBLANES and rows % (SUBLANES * st) == 0 and r0 % (SUBLANES * st) == 0
    first_sublane = lax.broadcasted_iota(jnp.int32, (SUBLANES, LANES), 0) == 0
    for lt in range(x_ref.shape[0]):
        ln = slice(lt * LANES, (lt + 1) * LANES)
        w = [jnp.broadcast_to(w_ref[k:k + 1, ln], (SUBLANES, LANES)) for k in range(taps)]
        bias = None if b_ref is None else jnp.broadcast_to(b_ref[:, ln], (SUBLANES, LANES))
        for a in range(r0, r0 + rows, SUBLANES * st):
            v = [x_ref[lt, pl.ds(a + j, SUBLANES, stride=st), :] for j in range(st)]
            hist = []
            for d in range(1, taps):
                before = tail_ref[lt, SUBLANES - d:SUBLANES - d + 1, :] if a == 0 else x_ref[lt, a - d:a - d + 1, :]
                hist.append(jnp.where(first_sublane, before, pltpu.roll(v[st - d], 1, 0)))
            for j in range(st):
                acc = w[taps - 1] * v[j]
                for d in range(1, taps):
                    acc = acc + w[taps - 1 - d] * (v[j - d] if j >= d else hist[d - j - 1])
                if bias is not None:
                    acc = acc + bias
                out_ref[lt, pl.ds(a + j, SUBLANES, stride=st), :] = _silu(acc)


def _save_tail(x_ref, tail_ref):
    rows = x_ref.shape[1]
    for lt in range(x_ref.shape[0]):
        tail_ref[lt] = x_ref[lt, rows - SUBLANES:rows, :]


def _lanes(ref, r0, rows):
    return jnp.concatenate([ref[lt, r0:r0 + rows, :] for lt in range(ref.shape[0])], axis=1)


def _spread(main, extra):
    out = list(main)
    for k, item in enumerate(extra):
        out.insert((k + 1) * len(main) // (len(extra) + 1) + k, item)
    return out


def _emit(groups, side):
    done = 0
    for gi, grp in enumerate(groups):
        grp()
        upto = (gi + 1) * len(side) // len(groups)
        for item in side[done:upto]:
            item()
        done = upto


def _inproj_body(x_ref, g_ref, w_ref, ws_ref, o_ref, os_ref, h_ref):
    @pl.when(pl.program_id(1) == 0)
    def _():
        hb = _rms(x_ref[...], g_ref[...]).astype(BF16)
        h_ref[...] = hb
        os_ref[...] = _dot(hb, ws_ref[...])

    res = _dot(h_ref[...], w_ref[...]).astype(o_ref.dtype)
    for lt in range(o_ref.shape[0]):
        o_ref[lt] = res[:, lt * LANES:(lt + 1) * LANES]


def _layer_block(layer, *tail, col=0):
    index = (layer,) + (0,) * (len(tail) - 1) + (col,)
    return pl.BlockSpec((None,) + tail, lambda *_: index)


def _inproj(x, g, w_big, w_small, layer, proj_dtype, tm=1024, tn=2176):
    n = x.shape[0]
    return pl.pallas_call(
        _inproj_body,
        grid=(n // tm, PROJ_COLS // tn),
        in_specs=[
            pl.BlockSpec((tm, D_MODEL), lambda i, j: (i, 0)),
            _layer_block(layer, 1, D_MODEL),
            pl.BlockSpec((None, D_MODEL, tn), lambda i, j: (layer, 0, j)),
            _layer_block(layer, D_MODEL, LANES),
        ],
        out_specs=[
            pl.BlockSpec((tn // LANES, tm, LANES), lambda i, j: (j, i, 0)),
            pl.BlockSpec((tm, LANES), lambda i, j: (i, 0)),
        ],
        out_shape=[
            jax.ShapeDtypeStruct((PROJ_COLS // LANES, n, LANES), proj_dtype),
            jax.ShapeDtypeStruct((n, LANES), F32),
        ],
        scratch_shapes=[pltpu.VMEM((tm, D_MODEL), BF16)],
        compiler_params=pltpu.CompilerParams(
            dimension_semantics=("parallel", "arbitrary"), vmem_limit_bytes=VMEM_LIMIT),
        name="inproj",
    )(x, g, w_big, w_small)


def _ssd_body(xs_ref, bc_ref, z_ref, sm_ref, cwx_ref, cwbc_ref, cbx_ref, cbbc_ref, dtb_ref, alog_ref,
              dsk_ref, ng_ref, e_ref, o_ref, xtail, bctail, xc, bcc, state, *, rows_blk):
    q = CHUNK
    gw = SSM_D_INNER // SSM_GROUPS

    @pl.when(pl.program_id(1) == 0)
    def _():
        xtail[...] = jnp.zeros(xtail.shape, F32)
        bctail[...] = jnp.zeros(bctail.shape, F32)
        state[...] = jnp.zeros(state.shape, F32)

    _conv_silu_tiles(xs_ref, xtail, cwx_ref, cbx_ref, xc, 0, rows_blk, SSM_CONV)
    _conv_silu_tiles(bc_ref, bctail, cwbc_ref, cbbc_ref, bcc, 0, rows_blk, SSM_CONV)
    _save_tail(xs_ref, xtail)
    _save_tail(bc_ref, bctail)

    lane = lax.broadcasted_iota(jnp.int32, (q, LANES), 1)
    rowi = lax.broadcasted_iota(jnp.int32, (q, LANES), 0)
    lo_half = lane < q
    causal2 = rowi >= (lane % q)
    li = lax.broadcasted_iota(jnp.int32, (q, q), 0)
    lj = lax.broadcasted_iota(jnp.int32, (q, q), 1)
    l_incl = (lj <= li).astype(BF16)
    u2 = (rowi <= (lane % q)).astype(BF16)
    r2 = lax.broadcasted_iota(jnp.int32, (2 * q, LANES), 0)
    c2 = lax.broadcasted_iota(jnp.int32, (2 * q, LANES), 1)
    blockdiag = (r2 < q) == (c2 < q)
    dt_lane = lax.broadcasted_iota(jnp.int32, (q, LANES), 1) < SSM_HEADS

    a_vec = -jnp.exp(alog_ref[...])
    e_mat = e_ref[...]

    for c in range(rows_blk // q):
        r0 = c * q
        xcv = _lanes(xc, r0, q)
        dt = jnp.where(dt_lane, _softplus(sm_ref[r0:r0 + q, :] + dtb_ref[...]), 0.0)
        adt = dt * a_vec
        parts = _split3(adt)
        acum = _dot(l_incl, parts[0]) + _dot(l_incl, parts[1]) + _dot(l_incl, parts[2])
        acum_t = _dot_ta(parts[0], u2) + _dot_ta(parts[1], u2) + _dot_ta(parts[2], u2)
        ea = jnp.exp(acum)
        ds = jnp.exp(acum[q - 1:q, :] - acum)
        ex = _expand(jnp.concatenate([dt, ea, ds], axis=0), e_mat)
        dt_e, ea_e, ds_e = ex[0:q], ex[q:2 * q], ex[2 * q:3 * q]
        xdt = xcv * dt_e
        xds = (xdt * ds_e).astype(BF16)
        y_parts = []
        for g in range(SSM_GROUPS):
            bg = bcc[g, r0:r0 + q, :].astype(BF16)
            cg = bcc[SSM_GROUPS + g, r0:r0 + q, :].astype(BF16)
            sc2 = _dot_tb(cg, jnp.concatenate([bg, bg], axis=0))
            st = state[:, g * gw:(g + 1) * gw]
            yoff = _dot(cg, st.astype(BF16)) * ea_e[:, g * gw:(g + 1) * gw]
            for pr in range(gw // LANES):
                h0 = g * (SSM_HEADS // SSM_GROUPS) + 2 * pr
                l0 = g * gw + pr * LANES
                col = jnp.where(lo_half, acum[:, h0:h0 + 1], acum[:, h0 + 1:h0 + 2])
                row = jnp.where(lo_half, acum_t[h0:h0 + 1, :], acum_t[h0 + 1:h0 + 2, :])
                lm = jnp.exp(jnp.where(causal2, col - row, -jnp.inf))
                pm = (sc2 * lm).astype(BF16)
                xp = xdt[:, l0:l0 + LANES].astype(BF16)
                rhs = jnp.where(blockdiag, jnp.concatenate([xp, xp], axis=0), jnp.zeros((), BF16))
                y_parts.append(_dot(pm, rhs) + yoff[:, pr * LANES:(pr + 1) * LANES])
            contrib = _dot_ta(bg, xds[:, g * gw:(g + 1) * gw])
            state[:, g * gw:(g + 1) * gw] = st * ea_e[q - 1:q, g * gw:(g + 1) * gw] + contrib
        y = jnp.concatenate(y_parts, axis=1) + dsk_ref[...] * xcv
        yz = y * _silu(_lanes(z_ref, r0, q).astype(F32))
        outs = []
        for g in range(SSM_GROUPS):
            part = yz[:, g * gw:(g + 1) * gw]
            outs.append(part * lax.rsqrt(jnp.mean(part * part, axis=-1, keepdims=True) + EPS))
        o_ref[r0:r0 + q, :] = (jnp.concatenate(outs, axis=1) * ng_ref[...]).astype(o_ref.dtype)


def _ssd(proj, small, conv_w, conv_b, dtb, alog, dsk, ng, e_mat, layer, batch, seq, rows_blk=512):
    nt = seq // rows_blk
    n = batch * seq
    bcw = 2 * SSM_GROUPS * SSM_STATE
    xt = SSM_D_INNER // LANES
    bt = bcw // LANES
    row = lambda b, t: b * nt + t
    full = lambda shape: pl.BlockSpec(shape, lambda b, t: (0,) * len(shape))
    return pl.pallas_call(
        functools.partial(_ssd_body, rows_blk=rows_blk),
        grid=(batch, nt),
        in_specs=[
            pl.BlockSpec((xt, rows_blk, LANES), lambda b, t: (COL_XS // SSM_D_INNER, row(b, t), 0)),
            pl.BlockSpec((bt, rows_blk, LANES), lambda b, t: (COL_BC // bcw, row(b, t), 0)),
            pl.BlockSpec((xt, rows_blk, LANES), lambda b, t: (COL_Z // SSM_D_INNER, row(b, t), 0)),
            pl.BlockSpec((rows_blk, LANES), lambda b, t: (row(b, t), 0)),
            _layer_block(layer, SSM_CONV, SSM_D_INNER), _layer_block(layer, SSM_CONV, bcw, col=SSM_D_INNER // bcw),
            _layer_block(layer, 1, SSM_D_INNER), _layer_block(layer, 1, bcw, col=SSM_D_INNER // bcw),
            _layer_block(layer, 1, LANES), _layer_block(layer, 1, LANES),
            _layer_block(layer, 1, SSM_D_INNER), _layer_block(layer, 1, SSM_D_INNER),
            full((LANES, SSM_D_INNER)),
        ],
        out_specs=pl.BlockSpec((rows_blk, SSM_D_INNER), lambda b, t: (row(b, t), 0)),
        out_shape=jax.ShapeDtypeStruct((n, SSM_D_INNER), BF16),
        scratch_shapes=[
            pltpu.VMEM((xt, SUBLANES, LANES), F32),
            pltpu.VMEM((bt, SUBLANES, LANES), F32),
            pltpu.VMEM((xt, rows_blk, LANES), F32),
            pltpu.VMEM((bt, rows_blk, LANES), F32),
            pltpu.VMEM((SSM_STATE, SSM_D_INNER), F32),
        ],
        compiler_params=pltpu.CompilerParams(
            dimension_semantics=("parallel", "arbitrary"), vmem_limit_bytes=VMEM_LIMIT),
        name="ssd",
    )(proj, proj, proj, small, conv_w, conv_w, conv_b, conv_b, dtb, alog, dsk, ng, e_mat)


def _gdn_body(q_ref, k_ref, v_ref, gate_ref, sm_ref, cwq_ref, cwk_ref, cwv_ref, dtb_ref, alog_ref, ng_ref,
              o_ref,
              qtail, ktail, vtail, qc, kc, vc, gc_ref, gct_ref, beta_ref, eg_ref, ekd_ref,
              kbq_ref, kn_ref, dec_ref, m_ref, p_ref, qk_ref, rhs_ref, wq_ref, u_ref, kd_ref, vn_ref, os_ref,
              last_ref, s_ref,
              *, rows_blk):
    sup = 2 * CHUNK
    nsc = rows_blk // sup
    nh = GDN_HEADS

    @pl.when(pl.program_id(1) == 0)
    def _():
        qtail[...] = jnp.zeros(qtail.shape, F32)
        ktail[...] = jnp.zeros(ktail.shape, F32)
        vtail[...] = jnp.zeros(vtail.shape, F32)
        s_ref[...] = jnp.zeros(s_ref.shape, F32)

    def conv(sc, stream):
        src, tail, cw, dst = ((q_ref, qtail, cwq_ref, qc), (k_ref, ktail, cwk_ref, kc),
                              (v_ref, vtail, cwv_ref, vc))[stream]
        _conv_silu_tiles(src, tail, cw, None, dst, sc * sup, sup, GDN_CONV)

    ri = lax.broadcasted_iota(jnp.int32, (sup, sup), 0)
    ci = lax.broadcasted_iota(jnp.int32, (sup, sup), 1)
    same = (ri < CHUNK) == (ci < CHUNK)
    incl = same & (ci <= ri)
    strict = same & (ci < ri)
    l_bd = incl.astype(BF16)
    u_bd = (same & (ri <= ci)).astype(BF16)
    eye = (ri == ci).astype(F32)
    lane = lax.broadcasted_iota(jnp.int32, (sup, LANES), 1)
    g_lane = (lane >= SM_A) & (lane < SM_A + GDN_HEADS)
    first_chunk = lax.broadcasted_iota(jnp.int32, (sup, LANES), 0) < CHUNK

    a_vec = -jnp.exp(alog_ref[...])
    scale = GDN_DK ** -0.5

    def pair(sc, h):
        return sc * nh + h

    def prep(sc):
        r0 = sc * sup
        sm = sm_ref[r0:r0 + sup, :]
        gl = jnp.where(g_lane, a_vec * _softplus(sm + dtb_ref[...]), 0.0)
        parts = _split3(gl)
        gc = _dot(l_bd, parts[0]) + _dot(l_bd, parts[1]) + _dot(l_bd, parts[2])
        gc_ref[...] = gc
        gct_ref[...] = _dot_ta(parts[0], u_bd) + _dot_ta(parts[1], u_bd) + _dot_ta(parts[2], u_bd)
        eg = jnp.exp(gc)
        gc_last = jnp.where(first_chunk, gc[CHUNK - 1:CHUNK, :], gc[sup - 1:sup, :])
        beta_ref[...] = jax.nn.sigmoid(sm)
        eg_ref[...] = eg
        ekd_ref[...] = jnp.exp(gc_last - gc)
        for c in range(2):
            last_ref[2 * sc + c] = jnp.broadcast_to(eg[(c + 1) * CHUNK - 1:(c + 1) * CHUNK, :], (SUBLANES, LANES))

    def a_vector(sc, h):
        p = pair(sc, h)
        r0 = sc * sup
        ln = slice(h * GDN_DK, (h + 1) * GDN_DK)
        qh = qc[h, r0:r0 + sup, :]
        kh = kc[h, r0:r0 + sup, :]
        vh = vc[h, r0:r0 + sup, :]
        qs = qh * (lax.rsqrt(jnp.sum(qh * qh, axis=-1, keepdims=True) + EPS) * scale)
        kn = kh * lax.rsqrt(jnp.sum(kh * kh, axis=-1, keepdims=True) + EPS)
        b_h = jnp.broadcast_to(beta_ref[:, SM_B + h:SM_B + h + 1], (sup, LANES))
        eg_h = jnp.broadcast_to(eg_ref[:, SM_A + h:SM_A + h + 1], (sup, LANES))
        ekd_h = jnp.broadcast_to(ekd_ref[:, SM_A + h:SM_A + h + 1], (sup, LANES))
        kb = kn * b_h
        kbq_ref[p] = jnp.concatenate([kb, qs], axis=0).astype(BF16)
        kn_ref[p] = kn.astype(BF16)
        col = gc_ref[:, SM_A + h:SM_A + h + 1]
        row = gct_ref[SM_A + h:SM_A + h + 1, :]
        dec_ref[p] = jnp.exp(jnp.where(incl, col - row, -jnp.inf))
        rhs_ref[p] = jnp.concatenate([vh * b_h, kb * eg_h], axis=1).astype(BF16)
        qd = (qs * eg_h).astype(BF16)
        for c in range(2):
            wq_ref[2 * p + c, CHUNK:2 * CHUNK, :] = qd[c * CHUNK:(c + 1) * CHUNK]
        kd_ref[p] = (kn * ekd_h).astype(BF16)

    def a_matmul(sc):
        for h in range(nh):
            p = pair(sc, h)
            kq = _dot_tb(kbq_ref[p], kn_ref[p])
            dec = dec_ref[p]
            nmat = jnp.where(strict, -(kq[0:sup] * dec), 0.0)
            m_ref[p] = nmat.astype(BF16)
            p_ref[p] = eye + nmat
            qk_ref[p] = (kq[sup:2 * sup] * dec).astype(BF16)

    def square(p):
        mb = m_ref[p]
        m_ref[p] = _dot(mb, mb).astype(BF16)

    def accumulate(p):
        pv = p_ref[p]
        p_ref[p] = pv + _dot(pv.astype(BF16), m_ref[p])

    def phase_b_groups(sc):
        groups = []
        for _ in range(5):
            groups.append(lambda: [square(pair(sc, h)) for h in range(nh)])
            groups.append(lambda: [accumulate(pair(sc, h)) for h in range(nh)])
        return groups

    def phase_c(sc):
        for h in range(nh):
            p = pair(sc, h)
            uw = _dot(p_ref[p].astype(BF16), rhs_ref[p])
            u_ref[p] = uw[:, 0:GDN_DV]
            w = uw[:, GDN_DV:2 * GDN_DV].astype(BF16)
            for c in range(2):
                wq_ref[2 * p + c, 0:CHUNK, :] = w[c * CHUNK:(c + 1) * CHUNK]

    def d_apply(sc, c):
        rows = slice(c * CHUNK, (c + 1) * CHUNK)
        for h in range(nh):
            p = pair(sc, h)
            r = _dot(wq_ref[2 * p + c], s_ref[h].astype(BF16))
            vn_ref[p, rows, :] = (u_ref[p, rows, :] - r[0:CHUNK]).astype(BF16)
            os_ref[p, rows, :] = r[CHUNK:2 * CHUNK]

    def d_update(sc, c):
        rows = slice(c * CHUNK, (c + 1) * CHUNK)
        for h in range(nh):
            p = pair(sc, h)
            last = last_ref[2 * sc + c, 0:1, SM_A + h:SM_A + h + 1]
            s_ref[h] = s_ref[h] * last + _dot_ta(kd_ref[p, rows, :], vn_ref[p, rows, :])

    def phase_d_items(sc):
        items = []
        for c in range(2):
            items.append(functools.partial(d_apply, sc, c))
            items.append(functools.partial(d_update, sc, c))
        return items

    def phase_e(sc, h):
        p = pair(sc, h)
        r0 = sc * sup
        ln = slice(h * GDN_DK, (h + 1) * GDN_DK)
        o = os_ref[p] + _dot(qk_ref[p], vn_ref[p])
        o = o * lax.rsqrt(jnp.mean(o * o, axis=-1, keepdims=True) + EPS) * ng_ref[...]
        o = o * _silu(gate_ref[h, r0:r0 + sup, :].astype(F32))
        o_ref[r0:r0 + sup, ln] = o.astype(o_ref.dtype)

    def phase_e_items(sc):
        return [functools.partial(phase_e, sc, h) for h in range(nh)]

    def a_vector_items(sc):
        return ([functools.partial(prep, sc)] + [functools.partial(conv, sc, s) for s in range(3)]
                + [functools.partial(a_vector, sc, h) for h in range(nh)])

    for item in a_vector_items(0):
        item()
    a_matmul(0)
    for sc in range(nsc):
        side = _spread(a_vector_items(sc + 1) if sc + 1 < nsc else [],
                       phase_d_items(sc - 1) if sc >= 1 else [])
        _emit(phase_b_groups(sc), side)
        phase_c(sc)
        if sc + 1 < nsc:
            a_matmul(sc + 1)
        if sc >= 1:
            for item in phase_e_items(sc - 1):
                item()
    for item in phase_d_items(nsc - 1) + phase_e_items(nsc - 1):
        item()
    _save_tail(q_ref, qtail)
    _save_tail(k_ref, ktail)
    _save_tail(v_ref, vtail)


def _gdn(proj, small, conv_w, dtb, alog, ng, layer, batch, seq, rows_blk=512):
    nt = seq // rows_blk
    n = batch * seq
    width = GDN_HEADS * GDN_DK
    sup = 2 * CHUNK
    npair = (rows_blk // sup) * GDN_HEADS
    row = lambda b, t: b * nt + t
    colblk = lambda col: pl.BlockSpec((GDN_HEADS, rows_blk, LANES), lambda b, t: (col // width, row(b, t), 0))
    return pl.pallas_call(
        functools.partial(_gdn_body, rows_blk=rows_blk),
        grid=(batch, nt),
        in_specs=[
            colblk(COL_Q), colblk(COL_K), colblk(COL_V), colblk(COL_GATE),
            pl.BlockSpec((rows_blk, LANES), lambda b, t: (row(b, t), 0)),
            _layer_block(layer, GDN_CONV, width, col=0), _layer_block(layer, GDN_CONV, width, col=1),
            _layer_block(layer, GDN_CONV, width, col=2),
            _layer_block(layer, 1, LANES), _layer_block(layer, 1, LANES), _layer_block(layer, 1, GDN_DV),
        ],
        out_specs=pl.BlockSpec((rows_blk, width), lambda b, t: (row(b, t), 0)),
        out_shape=jax.ShapeDtypeStruct((n, width), BF16),
        scratch_shapes=[
            pltpu.VMEM((GDN_HEADS, SUBLANES, LANES), F32),
            pltpu.VMEM((GDN_HEADS, SUBLANES, LANES), F32),
            pltpu.VMEM((GDN_HEADS, SUBLANES, LANES), F32),
            pltpu.VMEM((GDN_HEADS, rows_blk, LANES), F32),
            pltpu.VMEM((GDN_HEADS, rows_blk, LANES), F32),
            pltpu.VMEM((GDN_HEADS, rows_blk, LANES), F32),
            pltpu.VMEM((sup, LANES), F32),
            pltpu.VMEM((LANES, sup), F32),
            pltpu.VMEM((sup, LANES), F32),
            pltpu.VMEM((sup, LANES), F32),
            pltpu.VMEM((sup, LANES), F32),
            pltpu.VMEM((npair, 2 * sup, GDN_DK), BF16),
            pltpu.VMEM((npair, sup, GDN_DK), BF16),
            pltpu.VMEM((npair, sup, sup), F32),
            pltpu.VMEM((npair, sup, sup), BF16),
            pltpu.VMEM((npair, sup, sup), F32),
            pltpu.VMEM((npair, sup, sup), BF16),
            pltpu.VMEM((npair, sup, 2 * GDN_DV), BF16),
            pltpu.VMEM((2 * npair, sup, GDN_DK), BF16),
            pltpu.VMEM((npair, sup, GDN_DV), F32),
            pltpu.VMEM((npair, sup, GDN_DK), BF16),
            pltpu.VMEM((npair, sup, GDN_DV), BF16),
            pltpu.VMEM((npair, sup, GDN_DV), F32),
            pltpu.VMEM((2 * (rows_blk // sup), SUBLANES, LANES), F32),
            pltpu.VMEM((GDN_HEADS, GDN_DK, GDN_DV), F32),
        ],
        compiler_params=pltpu.CompilerParams(
            dimension_semantics=("parallel", "arbitrary"), vmem_limit_bytes=VMEM_LIMIT),
        name="gdn",
    )(proj, proj, proj, proj, small, conv_w, conv_w, conv_w, dtb, alog, ng)


def _mix_body(x_ref, ys_ref, yg_ref, ms_ref, mg_ref, ws_ref, wg_ref, wo_ref, g_ref, o_ref):
    a = _dot(ys_ref[...], ws_ref[...])
    b = _dot(yg_ref[...], wg_ref[...])
    rows = x_ref.shape[0]
    ms = _lanes(ms_ref, 0, rows).astype(F32)
    mg = _lanes(mg_ref, 0, rows).astype(F32)
    mixed = jax.nn.sigmoid(ms) * a + jax.nn.sigmoid(mg) * b
    out = _dot(mixed.astype(BF16), wo_ref[...])
    o_ref[...] = x_ref[...] + _rms(out, g_ref[...])


def _mix(x, y_ssm, y_gdn, proj, w_s, w_g, w_o, g, layer, tm=512):
    n = x.shape[0]
    rowblk = lambda col: pl.BlockSpec((tm, D_MODEL), lambda i: (i, col // D_MODEL))
    slabs = lambda col: pl.BlockSpec((D_MODEL // LANES, tm, LANES), lambda i: (col // D_MODEL, i, 0))
    weight = _layer_block(layer, D_MODEL, D_MODEL)
    return pl.pallas_call(
        _mix_body,
        grid=(n // tm,),
        in_specs=[rowblk(0), rowblk(0), rowblk(0), slabs(COL_MGS), slabs(COL_MGG),
                  weight, weight, weight, _layer_block(layer, 1, D_MODEL)],
        out_specs=rowblk(0),
        out_shape=jax.ShapeDtypeStruct((n, D_MODEL), F32),
        compiler_params=pltpu.CompilerParams(
            dimension_semantics=("parallel",), vmem_limit_bytes=VMEM_LIMIT),
        name="mix",
    )(x, y_ssm, y_gdn, proj, proj, w_s, w_g, w_o, g)


def _ffn_body(x_ref, xp_ref, gpre_ref, wg_ref, wu_ref, wd_ref, cw_ref, cb_ref, gpost_ref, o_ref,
              h_ref, gbuf, acc_ref, *, tm, tf, tiles_per_seq):
    i = pl.program_id(0)
    j = pl.program_id(1)

    @pl.when(j == 0)
    def _():
        h_ref[FFN_HALO:FFN_HALO + tm, :] = _rms(x_ref[...], gpre_ref[...]).astype(BF16)
        h_ref[0:FFN_HALO, :] = _rms(xp_ref[...], gpre_ref[...]).astype(BF16)
        acc_ref[...] = jnp.zeros(acc_ref.shape, F32)

    seq_start = i % tiles_per_seq == 0

    def gate_up(s):
        cs = slice(s * MXU_COLS, (s + 1) * MXU_COLS)
        g = _dot(h_ref[...], wg_ref[:, cs])
        gbuf[FFN_HALO:FFN_HALO + tm, cs] = g[FFN_HALO:FFN_HALO + tm]
        gbuf[0:FFN_HALO, cs] = jnp.where(seq_start, 0.0, g[0:FFN_HALO])
        return _dot(h_ref[FFN_HALO:FFN_HALO + tm, :], wu_ref[:, cs])

    def act_down(s, up):
        cs = slice(s * MXU_COLS, (s + 1) * MXU_COLS)
        conv = cb_ref[:, cs]
        for k in range(FFN_CONV):
            off = FFN_HALO - (FFN_CONV - 1) + k
            conv = conv + cw_ref[k:k + 1, cs] * gbuf[off:off + tm, cs]
        acc_ref[...] += _dot((_gelu_tanh(conv) * up).astype(BF16), wd_ref[cs, :])

    ns = tf // MXU_COLS
    up = gate_up(0)
    for s in range(ns):
        up_next = gate_up(s + 1) if s + 1 < ns else None
        act_down(s, up)
        up = up_next

    @pl.when(j == pl.num_programs(1) - 1)
    def _():
        o_ref[...] = x_ref[...] + _rms(acc_ref[...], gpost_ref[...])


def _ffn(x, gpre, w_gate, w_up, w_down, cw, cb, gpost, layer, seq, tm=1024, tf=512):
    n = x.shape[0]
    hb = tm // FFN_HALO
    cols = lambda rows: pl.BlockSpec((None, rows, tf), lambda i, j: (layer, 0, j))
    return pl.pallas_call(
        functools.partial(_ffn_body, tm=tm, tf=tf, tiles_per_seq=seq // tm),
        grid=(n // tm, D_FF // tf),
        in_specs=[
            pl.BlockSpec((tm, D_MODEL), lambda i, j: (i, 0)),
            pl.BlockSpec((FFN_HALO, D_MODEL), lambda i, j: (jnp.maximum(i * hb - 1, 0), 0)),
            _layer_block(layer, 1, D_MODEL),
            cols(D_MODEL), cols(D_MODEL),
            pl.BlockSpec((None, tf, D_MODEL), lambda i, j: (layer, j, 0)),
            cols(FFN_CONV), cols(1),
            _layer_block(layer, 1, D_MODEL),
        ],
        out_specs=pl.BlockSpec((tm, D_MODEL), lambda i, j: (i, 0)),
        out_shape=jax.ShapeDtypeStruct((n, D_MODEL), F32),
        scratch_shapes=[
            pltpu.VMEM((FFN_HALO + tm, D_MODEL), BF16),
            pltpu.VMEM((FFN_HALO + tm, tf), F32),
            pltpu.VMEM((tm, D_MODEL), F32),
        ],
        compiler_params=pltpu.CompilerParams(
            dimension_semantics=("parallel", "arbitrary"), vmem_limit_bytes=VMEM_LIMIT),
        name="ffn",
    )(x, x, gpre, w_gate, w_up, w_down, cw, cb, gpost)


def _ple_body(x_ref, p_ref, wg_ref, wp_ref, g_ref, o_ref):
    x = x_ref[...]
    e = jax.nn.sigmoid(_dot(x.astype(BF16), wg_ref[...])) * _dot(p_ref[...].astype(BF16), wp_ref[...])
    o_ref[...] = x + _rms(e, g_ref[...])


def _ple(x, p_all, layer, w_gate, w_proj, g, tm=1024):
    n = x.shape[0]
    return pl.pallas_call(
        _ple_body,
        grid=(n // tm,),
        in_specs=[
            pl.BlockSpec((tm, D_MODEL), lambda i: (i, 0)),
            pl.BlockSpec((None, tm, PLE_DIM), lambda i: (layer, i, 0)),
            _layer_block(layer, D_MODEL, D_MODEL),
            _layer_block(layer, PLE_DIM, D_MODEL),
            _layer_block(layer, 1, D_MODEL),
        ],
        out_specs=pl.BlockSpec((tm, D_MODEL), lambda i: (i, 0)),
        out_shape=jax.ShapeDtypeStruct((n, D_MODEL), F32),
        compiler_params=pltpu.CompilerParams(
            dimension_semantics=("parallel",), vmem_limit_bytes=VMEM_LIMIT),
        name="ple",
    )(x, p_all, w_gate, w_proj, g)


def _pad_lanes(v, offset):
    depth, heads = v.shape
    return jnp.zeros((depth, 1, LANES), F32).at[:, 0, offset:offset + heads].set(v.astype(F32))


def _expansion(offset, heads, width):
    r = jnp.arange(LANES)[:, None]
    c = jnp.arange(heads * width)[None, :]
    return (r == offset + c // width).astype(BF16)


def _split_w_in(w):
    i0 = 0
    z = w[..., i0:i0 + 1024]; i0 += 1024
    xs = w[..., i0:i0 + 1024]; i0 += 1024
    bc = w[..., i0:i0 + 512]; i0 += 512
    dt = w[..., i0:i0 + 16]; i0 += 16
    qkv = w[..., i0:i0 + 3072]; i0 += 3072
    gate = w[..., i0:i0 + 1024]; i0 += 1024
    b = w[..., i0:i0 + 8]; i0 += 8
    a = w[..., i0:i0 + 8]; i0 += 8
    mg = w[..., i0:i0 + 2048]
    big = jnp.concatenate([qkv, xs, z, gate, mg, bc], axis=-1).astype(BF16)
    small = jnp.concatenate([dt, b, a, jnp.zeros(w.shape[:-1] + (LANES - 32,), w.dtype)], axis=-1).astype(BF16)
    return big, small


PROJ_DTYPE = F32


def kernel(x, p, g_mix_pre, w_in, conv_ssm_w, conv_ssm_b, ssm_dt_bias, ssm_a_log, ssm_d, ssm_norm_g,
           conv_gdn_w, gdn_dt_bias, gdn_a_log, gdn_norm_g, w_br_ssm, w_br_gdn, w_out, g_mix_post,
           g_ffn_pre, w_ffn_gate, w_ffn_up, conv_ffn_w, conv_ffn_b, w_ffn_down, g_ffn_post,
           w_ple_gate, w_ple_proj, g_ple_post):
    batch, seq, d = x.shape
    depth = w_in.shape[0]
    n = batch * seq
    assert d == D_MODEL and seq % 1024 == 0
    xf = x.reshape(n, d)
    p_all = p.reshape(depth, n, PLE_DIM)

    rows = lambda v: v.astype(F32).reshape(depth, 1, -1)
    bf16 = lambda w: w.astype(BF16)
    w_big, w_small = _split_w_in(w_in)
    g_mix_pre, g_mix_post, g_ffn_pre, g_ffn_post, g_ple_post = map(
        rows, (g_mix_pre, g_mix_post, g_ffn_pre, g_ffn_post, g_ple_post))
    conv_ssm_w, conv_gdn_w, conv_ffn_w = (w.astype(F32) for w in (conv_ssm_w, conv_gdn_w, conv_ffn_w))
    conv_ssm_b, conv_ffn_b = rows(conv_ssm_b), rows(conv_ffn_b)
    ssm_dtb, ssm_alog = _pad_lanes(ssm_dt_bias, SM_DT), _pad_lanes(ssm_a_log, SM_DT)
    ssm_skip, ssm_norm_g = rows(jnp.repeat(ssm_d, SSM_HEAD_DIM, axis=1)), rows(ssm_norm_g)
    gdn_dtb, gdn_alog, gdn_norm_g = _pad_lanes(gdn_dt_bias, SM_A), _pad_lanes(gdn_a_log, SM_A), rows(gdn_norm_g)
    w_br_ssm, w_br_gdn, w_out, w_ffn_gate, w_ffn_up, w_ffn_down, w_ple_gate, w_ple_proj = map(
        bf16, (w_br_ssm, w_br_gdn, w_out, w_ffn_gate, w_ffn_up, w_ffn_down, w_ple_gate, w_ple_proj))
    e_ssd = _expansion(SM_DT, SSM_HEADS, SSM_HEAD_DIM)

    for i in range(depth):
        proj, small = _inproj(xf, g_mix_pre, w_big, w_small, i, PROJ_DTYPE)
        y_ssm = _ssd(proj, small, conv_ssm_w, conv_ssm_b, ssm_dtb, ssm_alog, ssm_skip, ssm_norm_g, e_ssd,
                     i, batch, seq)
        y_gdn = _gdn(proj, small, conv_gdn_w, gdn_dtb, gdn_alog, gdn_norm_g, i, batch, seq)
        xf = _mix(xf, y_ssm, y_gdn, proj, w_br_ssm, w_br_gdn, w_out, g_mix_post, i)
        xf = _ffn(xf, g_ffn_pre, w_ffn_gate, w_ffn_up, w_ffn_down, conv_ffn_w, conv_ffn_b, g_ffn_post, i, seq)
        xf = _ple(xf, p_all, i, w_ple_gate, w_ple_proj, g_ple_post)
    return xf.reshape(batch, seq, d)
```

```python
import functools

import jax
import jax.numpy as jnp
from jax import lax
from jax.experimental import pallas as pl
from jax.experimental.pallas import tpu as pltpu

F32 = jnp.float32
BF16 = jnp.bfloat16
EPS = 1e-6

D_MODEL = 1024
CHUNK = 64
SSM_HEADS = 16
SSM_HEAD_DIM = 64
SSM_D_INNER = 1024
SSM_GROUPS = 2
SSM_STATE = 128
SSM_CONV = 4
GDN_HEADS = 8
GDN_DK = 128
GDN_DV = 128
GDN_CONV = 4
D_FF = 4096
FFN_CONV = 3
PLE_DIM = 256

LANES = 128
SUBLANES = 8
MXU_COLS = 256
CONV_STRIDE = 4
FFN_HALO = 16
VMEM_LIMIT = 48 * 1024 * 1024

COL_Q, COL_K, COL_V, COL_XS, COL_Z, COL_GATE, COL_MGS, COL_MGG, COL_BC = (
    0, 1024, 2048, 3072, 4096, 5120, 6144, 7168, 8192)
PROJ_COLS = 8704
SM_DT, SM_B, SM_A = 0, 16, 24


def _rms(x, g):
    return x * lax.rsqrt(jnp.mean(x * x, axis=-1, keepdims=True) + EPS) * g


def _silu(x):
    return x * jax.nn.sigmoid(x)


def _softplus(x):
    return jnp.maximum(x, 0.0) + jnp.log1p(jnp.exp(-jnp.abs(x)))


def _gelu_tanh(x):
    c = 0.7978845608028654
    return 0.5 * x * (1.0 + jnp.tanh(c * (x + 0.044715 * (x * x * x))))


def _split2(v):
    hi = v.astype(BF16)
    lo = (v - hi.astype(F32)).astype(BF16)
    return hi, lo


def _split3(v):
    hi = v.astype(BF16)
    r = v - hi.astype(F32)
    mid = r.astype(BF16)
    lo = (r - mid.astype(F32)).astype(BF16)
    return hi, mid, lo


def _dot(a, b):
    return jnp.dot(a, b, preferred_element_type=F32)


def _dot_tb(a, b):
    return lax.dot_general(a, b, (((1,), (1,)), ((), ())), preferred_element_type=F32)


def _dot_ta(a, b):
    return lax.dot_general(a, b, (((0,), (0,)), ((), ())), preferred_element_type=F32)


def _expand(v, e):
    hi, lo = _split2(v)
    return _dot(hi, e) + _dot(lo, e)


def _conv_silu_tiles(x_ref, tail_ref, w_ref, b_ref, out_ref, r0, rows, taps):
    st = CONV_STRIDE
    assert taps - 1 <= st <= SUBLANES and rows % (SUBLANES * st) == 0 and r0 % (SUBLANES * st) == 0
    first_sublane = lax.broadcasted_iota(jnp.int32, (SUBLANES, LANES), 0) == 0
    for lt in range(x_ref.shape[0]):
        ln = slice(lt * LANES, (lt + 1) * LANES)
        w = [jnp.broadcast_to(w_ref[k:k + 1, ln], (SUBLANES, LANES)) for k in range(taps)]
        bias = None if b_ref is None else jnp.broadcast_to(b_ref[:, ln], (SUBLANES, LANES))
        for a in range(r0, r0 + rows, SUBLANES * st):
            v = [x_ref[lt, pl.ds(a + j, SUBLANES, stride=st), :] for j in range(st)]
            hist = []
            for d in range(1, taps):
                before = tail_ref[lt, SUBLANES - d:SUBLANES - d + 1, :] if a == 0 else x_ref[lt, a - d:a - d + 1, :]
                hist.append(jnp.where(first_sublane, before, pltpu.roll(v[st - d], 1, 0)))
            for j in range(st):
                acc = w[taps - 1] * v[j]
                for d in range(1, taps):
                    acc = acc + w[taps - 1 - d] * (v[j - d] if j >= d else hist[d - j - 1])
                if bias is not None:
                    acc = acc + bias
                out_ref[lt, pl.ds(a + j, SUBLANES, stride=st), :] = _silu(acc)


def _save_tail(x_ref, tail_ref):
    rows = x_ref.shape[1]
    for lt in range(x_ref.shape[0]):
        tail_ref[lt] = x_ref[lt, rows - SUBLANES:rows, :]


def _lanes(ref, r0, rows):
    return jnp.concatenate([ref[lt, r0:r0 + rows, :] for lt in range(ref.shape[0])], axis=1)


def _spread(main, extra):
    out = list(main)
    for k, item in enumerate(extra):
        out.insert((k + 1) * len(main) // (len(extra) + 1) + k, item)
    return out


def _emit(groups, side):
    done = 0
    for gi, grp in enumerate(groups):
        grp()
        upto = (gi + 1) * len(side) // len(groups)
        for item in side[done:upto]:
            item()
        done = upto


def _inproj_body(x_ref, g_ref, w_ref, ws_ref, o_ref, os_ref, h_ref):
    @pl.when(pl.program_id(1) == 0)
    def _():
        hb = _rms(x_ref[...], g_ref[...]).astype(BF16)
        h_ref[...] = hb
        os_ref[...] = _dot(hb, ws_ref[...])

    res = _dot(h_ref[...], w_ref[...]).astype(o_ref.dtype)
    for lt in range(o_ref.shape[0]):
        o_ref[lt] = res[:, lt * LANES:(lt + 1) * LANES]


def _layer_block(layer, *tail, col=0):
    index = (layer,) + (0,) * (len(tail) - 1) + (col,)
    return pl.BlockSpec((None,) + tail, lambda *_: index)


def _inproj(x, g, w_big, w_small, layer, proj_dtype, tm=1024, tn=2176):
    n = x.shape[0]
    return pl.pallas_call(
        _inproj_body,
        grid=(n // tm, PROJ_COLS // tn),
        in_specs=[
            pl.BlockSpec((tm, D_MODEL), lambda i, j: (i, 0)),
            _layer_block(layer, 1, D_MODEL),
            pl.BlockSpec((None, D_MODEL, tn), lambda i, j: (layer, 0, j)),
            _layer_block(layer, D_MODEL, LANES),
        ],
        out_specs=[
            pl.BlockSpec((tn // LANES, tm, LANES), lambda i, j: (j, i, 0)),
            pl.BlockSpec((tm, LANES), lambda i, j: (i, 0)),
        ],
        out_shape=[
            jax.ShapeDtypeStruct((PROJ_COLS // LANES, n, LANES), proj_dtype),
            jax.ShapeDtypeStruct((n, LANES), F32),
        ],
        scratch_shapes=[pltpu.VMEM((tm, D_MODEL), BF16)],
        compiler_params=pltpu.CompilerParams(
            dimension_semantics=("parallel", "arbitrary"), vmem_limit_bytes=VMEM_LIMIT),
        name="inproj",
    )(x, g, w_big, w_small)


def _ssd_body(xs_ref, bc_ref, z_ref, sm_ref, cwx_ref, cwbc_ref, cbx_ref, cbbc_ref, dtb_ref, alog_ref,
              dsk_ref, ng_ref, e_ref, o_ref, xtail, bctail, xc, bcc, state, *, rows_blk):
    q = CHUNK
    gw = SSM_D_INNER // SSM_GROUPS

    @pl.when(pl.program_id(1) == 0)
    def _():
        xtail[...] = jnp.zeros(xtail.shape, F32)
        bctail[...] = jnp.zeros(bctail.shape, F32)
        state[...] = jnp.zeros(state.shape, F32)

    _conv_silu_tiles(xs_ref, xtail, cwx_ref, cbx_ref, xc, 0, rows_blk, SSM_CONV)
    _conv_silu_tiles(bc_ref, bctail, cwbc_ref, cbbc_ref, bcc, 0, rows_blk, SSM_CONV)
    _save_tail(xs_ref, xtail)
    _save_tail(bc_ref, bctail)

    lane = lax.broadcasted_iota(jnp.int32, (q, LANES), 1)
    rowi = lax.broadcasted_iota(jnp.int32, (q, LANES), 0)
    lo_half = lane < q
    causal2 = rowi >= (lane % q)
    li = lax.broadcasted_iota(jnp.int32, (q, q), 0)
    lj = lax.broadcasted_iota(jnp.int32, (q, q), 1)
    l_incl = (lj <= li).astype(BF16)
    u2 = (rowi <= (lane % q)).astype(BF16)
    r2 = lax.broadcasted_iota(jnp.int32, (2 * q, LANES), 0)
    c2 = lax.broadcasted_iota(jnp.int32, (2 * q, LANES), 1)
    blockdiag = (r2 < q) == (c2 < q)
    dt_lane = lax.broadcasted_iota(jnp.int32, (q, LANES), 1) < SSM_HEADS

    a_vec = -jnp.exp(alog_ref[...])
    e_mat = e_ref[...]

    nc = rows_blk // q
    live = [dict() for _ in range(nc)]

    def b_of(c, g):
        return bcc[g, c * q:(c + 1) * q, :].astype(BF16)

    def c_of(c, g):
        return bcc[SSM_GROUPS + g, c * q:(c + 1) * q, :].astype(BF16)

    def cumulative(c):
        v = live[c]
        v["dt"] = jnp.where(dt_lane, _softplus(sm_ref[c * q:(c + 1) * q, :] + dtb_ref[...]), 0.0)
        parts = _split3(v["dt"] * a_vec)
        v["acum"] = _dot(l_incl, parts[0]) + _dot(l_incl, parts[1]) + _dot(l_incl, parts[2])
        v["acum_t"] = _dot_ta(parts[0], u2) + _dot_ta(parts[1], u2) + _dot_ta(parts[2], u2)

    def expand(c):
        v = live[c]
        ea = jnp.exp(v["acum"])
        ds = jnp.exp(v["acum"][q - 1:q, :] - v["acum"])
        ex = _expand(jnp.concatenate([v.pop("dt"), ea, ds], axis=0), e_mat)
        v["ea_e"] = ex[q:2 * q]
        v["xdt"] = _lanes(xc, c * q, q) * ex[0:q]
        v["xds"] = (v["xdt"] * ex[2 * q:3 * q]).astype(BF16)

    def in_chunk(c):
        v = live[c]
        xds = v.pop("xds")
        v["contrib"] = [_dot_ta(b_of(c, g), xds[:, g * gw:(g + 1) * gw]) for g in range(SSM_GROUPS)]
        v["y"] = []
        for g in range(SSM_GROUPS):
            bg = b_of(c, g)
            sc2 = _dot_tb(c_of(c, g), jnp.concatenate([bg, bg], axis=0))
            for pr in range(gw // LANES):
                h0 = g * (SSM_HEADS // SSM_GROUPS) + 2 * pr
                l0 = g * gw + pr * LANES
                col = jnp.where(lo_half, v["acum"][:, h0:h0 + 1], v["acum"][:, h0 + 1:h0 + 2])
                row = jnp.where(lo_half, v["acum_t"][h0:h0 + 1, :], v["acum_t"][h0 + 1:h0 + 2, :])
                lm = jnp.exp(jnp.where(causal2, col - row, -jnp.inf))
                pm = (sc2 * lm).astype(BF16)
                xp = v["xdt"][:, l0:l0 + LANES].astype(BF16)
                rhs = jnp.where(blockdiag, jnp.concatenate([xp, xp], axis=0), jnp.zeros((), BF16))
                v["y"].append(_dot(pm, rhs))
        del v["acum"], v["acum_t"], v["xdt"]

    def carry(c):
        v = live[c]
        y_parts = []
        for g in range(SSM_GROUPS):
            gl = slice(g * gw, (g + 1) * gw)
            st = state[:, gl]
            yoff = _dot(c_of(c, g), st.astype(BF16)) * v["ea_e"][:, gl]
            state[:, gl] = st * v["ea_e"][q - 1:q, gl] + v["contrib"][g]
            for pr in range(gw // LANES):
                y_parts.append(v["y"][g * (gw // LANES) + pr] + yoff[:, pr * LANES:(pr + 1) * LANES])
        y = jnp.concatenate(y_parts, axis=1) + dsk_ref[...] * _lanes(xc, c * q, q)
        yz = y * _silu(_lanes(z_ref, c * q, q).astype(F32))
        outs = []
        for g in range(SSM_GROUPS):
            part = yz[:, g * gw:(g + 1) * gw]
            outs.append(part * lax.rsqrt(jnp.mean(part * part, axis=-1, keepdims=True) + EPS))
        o_ref[c * q:(c + 1) * q, :] = (jnp.concatenate(outs, axis=1) * ng_ref[...]).astype(o_ref.dtype)
        live[c].clear()

    stages = (cumulative, expand, in_chunk, carry)
    for t in range(nc + len(stages) - 1):
        for k, stage in enumerate(stages):
            if 0 <= t - k < nc:
                stage(t - k)


def _ssd(proj, small, conv_w, conv_b, dtb, alog, dsk, ng, e_mat, layer, batch, seq, rows_blk=512):
    nt = seq // rows_blk
    n = batch * seq
    bcw = 2 * SSM_GROUPS * SSM_STATE
    xt = SSM_D_INNER // LANES
    bt = bcw // LANES
    row = lambda b, t: b * nt + t
    full = lambda shape: pl.BlockSpec(shape, lambda b, t: (0,) * len(shape))
    return pl.pallas_call(
        functools.partial(_ssd_body, rows_blk=rows_blk),
        grid=(batch, nt),
        in_specs=[
            pl.BlockSpec((xt, rows_blk, LANES), lambda b, t: (COL_XS // SSM_D_INNER, row(b, t), 0)),
            pl.BlockSpec((bt, rows_blk, LANES), lambda b, t: (COL_BC // bcw, row(b, t), 0)),
            pl.BlockSpec((xt, rows_blk, LANES), lambda b, t: (COL_Z // SSM_D_INNER, row(b, t), 0)),
            pl.BlockSpec((rows_blk, LANES), lambda b, t: (row(b, t), 0)),
            _layer_block(layer, SSM_CONV, SSM_D_INNER), _layer_block(layer, SSM_CONV, bcw, col=SSM_D_INNER // bcw),
            _layer_block(layer, 1, SSM_D_INNER), _layer_block(layer, 1, bcw, col=SSM_D_INNER // bcw),
            _layer_block(layer, 1, LANES), _layer_block(layer, 1, LANES),
            _layer_block(layer, 1, SSM_D_INNER), _layer_block(layer, 1, SSM_D_INNER),
            full((LANES, SSM_D_INNER)),
        ],
        out_specs=pl.BlockSpec((rows_blk, SSM_D_INNER), lambda b, t: (row(b, t), 0)),
        out_shape=jax.ShapeDtypeStruct((n, SSM_D_INNER), BF16),
        scratch_shapes=[
            pltpu.VMEM((xt, SUBLANES, LANES), F32),
            pltpu.VMEM((bt, SUBLANES, LANES), F32),
            pltpu.VMEM((xt, rows_blk, LANES), F32),
            pltpu.VMEM((bt, rows_blk, LANES), F32),
            pltpu.VMEM((SSM_STATE, SSM_D_INNER), F32),
        ],
        compiler_params=pltpu.CompilerParams(
            dimension_semantics=("parallel", "arbitrary"), vmem_limit_bytes=VMEM_LIMIT),
        name="ssd",
    )(proj, proj, proj, small, conv_w, conv_w, conv_b, conv_b, dtb, alog, dsk, ng, e_mat)


def _gdn_body(q_ref, k_ref, v_ref, gate_ref, sm_ref, cwq_ref, cwk_ref, cwv_ref, dtb_ref, alog_ref, ng_ref,
              o_ref,
              qtail, ktail, vtail, qc, kc, vc, gc_ref, gct_ref, beta_ref, eg_ref, ekd_ref,
              kbq_ref, kn_ref, dec_ref, m_ref, p_ref, qk_ref, rhs_ref, wq_ref, u_ref, kd_ref, vn_ref, os_ref,
              last_ref, s_ref,
              *, rows_blk):
    sup = 2 * CHUNK
    nsc = rows_blk // sup
    nh = GDN_HEADS

    @pl.when(pl.program_id(1) == 0)
    def _():
        qtail[...] = jnp.zeros(qtail.shape, F32)
        ktail[...] = jnp.zeros(ktail.shape, F32)
        vtail[...] = jnp.zeros(vtail.shape, F32)
        s_ref[...] = jnp.zeros(s_ref.shape, F32)

    def conv(sc, stream):
        src, tail, cw, dst = ((q_ref, qtail, cwq_ref, qc), (k_ref, ktail, cwk_ref, kc),
                              (v_ref, vtail, cwv_ref, vc))[stream]
        _conv_silu_tiles(src, tail, cw, None, dst, sc * sup, sup, GDN_CONV)

    ri = lax.broadcasted_iota(jnp.int32, (sup, sup), 0)
    ci = lax.broadcasted_iota(jnp.int32, (sup, sup), 1)
    same = (ri < CHUNK) == (ci < CHUNK)
    incl = same & (ci <= ri)
    strict = same & (ci < ri)
    l_bd = incl.astype(BF16)
    u_bd = (same & (ri <= ci)).astype(BF16)
    eye = (ri == ci).astype(F32)
    lane = lax.broadcasted_iota(jnp.int32, (sup, LANES), 1)
    g_lane = (lane >= SM_A) & (lane < SM_A + GDN_HEADS)
    first_chunk = lax.broadcasted_iota(jnp.int32, (sup, LANES), 0) < CHUNK

    a_vec = -jnp.exp(alog_ref[...])
    scale = GDN_DK ** -0.5

    def pair(sc, h):
        return sc * nh + h

    def prep(sc):
        r0 = sc * sup
        sm = sm_ref[r0:r0 + sup, :]
        gl = jnp.where(g_lane, a_vec * _softplus(sm + dtb_ref[...]), 0.0)
        parts = _split3(gl)
        gc = _dot(l_bd, parts[0]) + _dot(l_bd, parts[1]) + _dot(l_bd, parts[2])
        gc_ref[...] = gc
        gct_ref[...] = _dot_ta(parts[0], u_bd) + _dot_ta(parts[1], u_bd) + _dot_ta(parts[2], u_bd)
        eg = jnp.exp(gc)
        gc_last = jnp.where(first_chunk, gc[CHUNK - 1:CHUNK, :], gc[sup - 1:sup, :])
        beta_ref[...] = jax.nn.sigmoid(sm)
        eg_ref[...] = eg
        ekd_ref[...] = jnp.exp(gc_last - gc)
        for c in range(2):
            last_ref[2 * sc + c] = jnp.broadcast_to(eg[(c + 1) * CHUNK - 1:(c + 1) * CHUNK, :], (SUBLANES, LANES))

    def a_vector(sc, h):
        p = pair(sc, h)
        r0 = sc * sup
        ln = slice(h * GDN_DK, (h + 1) * GDN_DK)
        qh = qc[h, r0:r0 + sup, :]
        kh = kc[h, r0:r0 + sup, :]
        vh = vc[h, r0:r0 + sup, :]
        qs = qh * (lax.rsqrt(jnp.sum(qh * qh, axis=-1, keepdims=True) + EPS) * scale)
        kn = kh * lax.rsqrt(jnp.sum(kh * kh, axis=-1, keepdims=True) + EPS)
        b_h = jnp.broadcast_to(beta_ref[:, SM_B + h:SM_B + h + 1], (sup, LANES))
        eg_h = jnp.broadcast_to(eg_ref[:, SM_A + h:SM_A + h + 1], (sup, LANES))
        ekd_h = jnp.broadcast_to(ekd_ref[:, SM_A + h:SM_A + h + 1], (sup, LANES))
        kb = kn * b_h
        kbq_ref[p] = jnp.concatenate([kb, qs], axis=0).astype(BF16)
        kn_ref[p] = kn.astype(BF16)
        col = gc_ref[:, SM_A + h:SM_A + h + 1]
        row = gct_ref[SM_A + h:SM_A + h + 1, :]
        dec_ref[p] = jnp.exp(jnp.where(incl, col - row, -jnp.inf))
        rhs_ref[p] = jnp.concatenate([vh * b_h, kb * eg_h], axis=1).astype(BF16)
        qd = (qs * eg_h).astype(BF16)
        for c in range(2):
            wq_ref[2 * p + c, CHUNK:2 * CHUNK, :] = qd[c * CHUNK:(c + 1) * CHUNK]
        kd_ref[p] = (kn * ekd_h).astype(BF16)

    def a_matmul(sc):
        for h in range(nh):
            p = pair(sc, h)
            kq = _dot_tb(kbq_ref[p], kn_ref[p])
            dec = dec_ref[p]
            nmat = jnp.where(strict, -(kq[0:sup] * dec), 0.0)
            m_ref[p] = nmat.astype(BF16)
            p_ref[p] = eye + nmat
            qk_ref[p] = (kq[sup:2 * sup] * dec).astype(BF16)

    def square(p):
        mb = m_ref[p]
        m_ref[p] = _dot(mb, mb).astype(BF16)

    def accumulate(p):
        pv = p_ref[p]
        p_ref[p] = pv + _dot(pv.astype(BF16), m_ref[p])

    def phase_b_groups(sc):
        groups = []
        for _ in range(5):
            groups.append(lambda: [square(pair(sc, h)) for h in range(nh)])
            groups.append(lambda: [accumulate(pair(sc, h)) for h in range(nh)])
        return groups

    def phase_c(sc):
        for h in range(nh):
            p = pair(sc, h)
            uw = _dot(p_ref[p].astype(BF16), rhs_ref[p])
            u_ref[p] = uw[:, 0:GDN_DV]
            w = uw[:, GDN_DV:2 * GDN_DV].astype(BF16)
            for c in range(2):
                wq_ref[2 * p + c, 0:CHUNK, :] = w[c * CHUNK:(c + 1) * CHUNK]

    def d_apply(sc, c):
        rows = slice(c * CHUNK, (c + 1) * CHUNK)
        for h in range(nh):
            p = pair(sc, h)
            r = _dot(wq_ref[2 * p + c], s_ref[h].astype(BF16))
            vn_ref[p, rows, :] = (u_ref[p, rows, :] - r[0:CHUNK]).astype(BF16)
            os_ref[p, rows, :] = r[CHUNK:2 * CHUNK]

    def d_update(sc, c):
        rows = slice(c * CHUNK, (c + 1) * CHUNK)
        for h in range(nh):
            p = pair(sc, h)
            last = last_ref[2 * sc + c, 0:1, SM_A + h:SM_A + h + 1]
            s_ref[h] = s_ref[h] * last + _dot_ta(kd_ref[p, rows, :], vn_ref[p, rows, :])

    def phase_d_items(sc):
        items = []
        for c in range(2):
            items.append(functools.partial(d_apply, sc, c))
            items.append(functools.partial(d_update, sc, c))
        return items

    def phase_e(sc, h):
        p = pair(sc, h)
        r0 = sc * sup
        ln = slice(h * GDN_DK, (h + 1) * GDN_DK)
        o = os_ref[p] + _dot(qk_ref[p], vn_ref[p])
        o = o * lax.rsqrt(jnp.mean(o * o, axis=-1, keepdims=True) + EPS) * ng_ref[...]
        o = o * _silu(gate_ref[h, r0:r0 + sup, :].astype(F32))
        o_ref[r0:r0 + sup, ln] = o.astype(o_ref.dtype)

    def phase_e_items(sc):
        return [functools.partial(phase_e, sc, h) for h in range(nh)]

    def a_vector_items(sc):
        return ([functools.partial(prep, sc)] + [functools.partial(conv, sc, s) for s in range(3)]
                + [functools.partial(a_vector, sc, h) for h in range(nh)])

    for item in a_vector_items(0):
        item()
    a_matmul(0)
    for sc in range(nsc):
        side = _spread(a_vector_items(sc + 1) if sc + 1 < nsc else [],
                       phase_d_items(sc - 1) if sc >= 1 else [])
        _emit(phase_b_groups(sc), side)
        phase_c(sc)
        if sc + 1 < nsc:
            a_matmul(sc + 1)
        if sc >= 1:
            for item in phase_e_items(sc - 1):
                item()
    for item in phase_d_items(nsc - 1) + phase_e_items(nsc - 1):
        item()
    _save_tail(q_ref, qtail)
    _save_tail(k_ref, ktail)
    _save_tail(v_ref, vtail)


def _gdn(proj, small, conv_w, dtb, alog, ng, layer, batch, seq, rows_blk=512):
    nt = seq // rows_blk
    n = batch * seq
    width = GDN_HEADS * GDN_DK
    sup = 2 * CHUNK
    npair = (rows_blk // sup) * GDN_HEADS
    row = lambda b, t: b * nt + t
    colblk = lambda col: pl.BlockSpec((GDN_HEADS, rows_blk, LANES), lambda b, t: (col // width, row(b, t), 0))
    return pl.pallas_call(
        functools.partial(_gdn_body, rows_blk=rows_blk),
        grid=(batch, nt),
        in_specs=[
            colblk(COL_Q), colblk(COL_K), colblk(COL_V), colblk(COL_GATE),
            pl.BlockSpec((rows_blk, LANES), lambda b, t: (row(b, t), 0)),
            _layer_block(layer, GDN_CONV, width, col=0), _layer_block(layer, GDN_CONV, width, col=1),
            _layer_block(layer, GDN_CONV, width, col=2),
            _layer_block(layer, 1, LANES), _layer_block(layer, 1, LANES), _layer_block(layer, 1, GDN_DV),
        ],
        out_specs=pl.BlockSpec((rows_blk, width), lambda b, t: (row(b, t), 0)),
        out_shape=jax.ShapeDtypeStruct((n, width), BF16),
        scratch_shapes=[
            pltpu.VMEM((GDN_HEADS, SUBLANES, LANES), F32),
            pltpu.VMEM((GDN_HEADS, SUBLANES, LANES), F32),
            pltpu.VMEM((GDN_HEADS, SUBLANES, LANES), F32),
            pltpu.VMEM((GDN_HEADS, rows_blk, LANES), F32),
            pltpu.VMEM((GDN_HEADS, rows_blk, LANES), F32),
            pltpu.VMEM((GDN_HEADS, rows_blk, LANES), F32),
            pltpu.VMEM((sup, LANES), F32),
            pltpu.VMEM((LANES, sup), F32),
            pltpu.VMEM((sup, LANES), F32),
            pltpu.VMEM((sup, LANES), F32),
            pltpu.VMEM((sup, LANES), F32),
            pltpu.VMEM((npair, 2 * sup, GDN_DK), BF16),
            pltpu.VMEM((npair, sup, GDN_DK), BF16),
            pltpu.VMEM((npair, sup, sup), F32),
            pltpu.VMEM((npair, sup, sup), BF16),
            pltpu.VMEM((npair, sup, sup), F32),
            pltpu.VMEM((npair, sup, sup), BF16),
            pltpu.VMEM((npair, sup, 2 * GDN_DV), BF16),
            pltpu.VMEM((2 * npair, sup, GDN_DK), BF16),
            pltpu.VMEM((npair, sup, GDN_DV), F32),
            pltpu.VMEM((npair, sup, GDN_DK), BF16),
            pltpu.VMEM((npair, sup, GDN_DV), BF16),
            pltpu.VMEM((npair, sup, GDN_DV), F32),
            pltpu.VMEM((2 * (rows_blk // sup), SUBLANES, LANES), F32),
            pltpu.VMEM((GDN_HEADS, GDN_DK, GDN_DV), F32),
        ],
        compiler_params=pltpu.CompilerParams(
            dimension_semantics=("parallel", "arbitrary"), vmem_limit_bytes=VMEM_LIMIT),
        name="gdn",
    )(proj, proj, proj, proj, small, conv_w, conv_w, conv_w, dtb, alog, ng)


def _mix_body(x_ref, ys_ref, yg_ref, ms_ref, mg_ref, ws_ref, wg_ref, wo_ref, g_ref, o_ref):
    a = _dot(ys_ref[...], ws_ref[...])
    b = _dot(yg_ref[...], wg_ref[...])
    rows = x_ref.shape[0]
    ms = _lanes(ms_ref, 0, rows).astype(F32)
    mg = _lanes(mg_ref, 0, rows).astype(F32)
    mixed = jax.nn.sigmoid(ms) * a + jax.nn.sigmoid(mg) * b
    out = _dot(mixed.astype(BF16), wo_ref[...])
    o_ref[...] = x_ref[...] + _rms(out, g_ref[...])


def _mix(x, y_ssm, y_gdn, proj, w_s, w_g, w_o, g, layer, tm=512):
    n = x.shape[0]
    rowblk = lambda col: pl.BlockSpec((tm, D_MODEL), lambda i: (i, col // D_MODEL))
    slabs = lambda col: pl.BlockSpec((D_MODEL // LANES, tm, LANES), lambda i: (col // D_MODEL, i, 0))
    weight = _layer_block(layer, D_MODEL, D_MODEL)
    return pl.pallas_call(
        _mix_body,
        grid=(n // tm,),
        in_specs=[rowblk(0), rowblk(0), rowblk(0), slabs(COL_MGS), slabs(COL_MGG),
                  weight, weight, weight, _layer_block(layer, 1, D_MODEL)],
        out_specs=rowblk(0),
        out_shape=jax.ShapeDtypeStruct((n, D_MODEL), F32),
        compiler_params=pltpu.CompilerParams(
            dimension_semantics=("parallel",), vmem_limit_bytes=VMEM_LIMIT),
        name="mix",
    )(x, y_ssm, y_gdn, proj, proj, w_s, w_g, w_o, g)


def _ffn_body(x_ref, xp_ref, gpre_ref, wg_ref, wu_ref, wd_ref, cw_ref, cb_ref, gpost_ref, o_ref,
              h_ref, gbuf, acc_ref, *, tm, tf, tiles_per_seq):
    i = pl.program_id(0)
    j = pl.program_id(1)

    @pl.when(j == 0)
    def _():
        h_ref[FFN_HALO:FFN_HALO + tm, :] = _rms(x_ref[...], gpre_ref[...]).astype(BF16)
        h_ref[0:FFN_HALO, :] = _rms(xp_ref[...], gpre_ref[...]).astype(BF16)
        acc_ref[...] = jnp.zeros(acc_ref.shape, F32)

    seq_start = i % tiles_per_seq == 0

    def gate_up(s):
        cs = slice(s * MXU_COLS, (s + 1) * MXU_COLS)
        g = _dot(h_ref[...], wg_ref[:, cs])
        gbuf[FFN_HALO:FFN_HALO + tm, cs] = g[FFN_HALO:FFN_HALO + tm]
        gbuf[0:FFN_HALO, cs] = jnp.where(seq_start, 0.0, g[0:FFN_HALO])
        return _dot(h_ref[FFN_HALO:FFN_HALO + tm, :], wu_ref[:, cs])

    def act_down(s, up):
        cs = slice(s * MXU_COLS, (s + 1) * MXU_COLS)
        conv = cb_ref[:, cs]
        for k in range(FFN_CONV):
            off = FFN_HALO - (FFN_CONV - 1) + k
            conv = conv + cw_ref[k:k + 1, cs] * gbuf[off:off + tm, cs]
        acc_ref[...] += _dot((_gelu_tanh(conv) * up).astype(BF16), wd_ref[cs, :])

    ns = tf // MXU_COLS
    up = gate_up(0)
    for s in range(ns):
        up_next = gate_up(s + 1) if s + 1 < ns else None
        act_down(s, up)
        up = up_next

    @pl.when(j == pl.num_programs(1) - 1)
    def _():
        o_ref[...] = x_ref[...] + _rms(acc_ref[...], gpost_ref[...])


def _ffn(x, gpre, w_gate, w_up, w_down, cw, cb, gpost, layer, seq, tm=1024, tf=512):
    n = x.shape[0]
    hb = tm // FFN_HALO
    cols = lambda rows: pl.BlockSpec((None, rows, tf), lambda i, j: (layer, 0, j))
    return pl.pallas_call(
        functools.partial(_ffn_body, tm=tm, tf=tf, tiles_per_seq=seq // tm),
        grid=(n // tm, D_FF // tf),
        in_specs=[
            pl.BlockSpec((tm, D_MODEL), lambda i, j: (i, 0)),
            pl.BlockSpec((FFN_HALO, D_MODEL), lambda i, j: (jnp.maximum(i * hb - 1, 0), 0)),
            _layer_block(layer, 1, D_MODEL),
            cols(D_MODEL), cols(D_MODEL),
            pl.BlockSpec((None, tf, D_MODEL), lambda i, j: (layer, j, 0)),
            cols(FFN_CONV), cols(1),
            _layer_block(layer, 1, D_MODEL),
        ],
        out_specs=pl.BlockSpec((tm, D_MODEL), lambda i, j: (i, 0)),
        out_shape=jax.ShapeDtypeStruct((n, D_MODEL), F32),
        scratch_shapes=[
            pltpu.VMEM((FFN_HALO + tm, D_MODEL), BF16),
            pltpu.VMEM((FFN_HALO + tm, tf), F32),
            pltpu.VMEM((tm, D_MODEL), F32),
        ],
        compiler_params=pltpu.CompilerParams(
            dimension_semantics=("parallel", "arbitrary"), vmem_limit_bytes=VMEM_LIMIT),
        name="ffn",
    )(x, x, gpre, w_gate, w_up, w_down, cw, cb, gpost)


def _ple_body(x_ref, p_ref, wg_ref, wp_ref, g_ref, o_ref):
    x = x_ref[...]
    e = jax.nn.sigmoid(_dot(x.astype(BF16), wg_ref[...])) * _dot(p_ref[...].astype(BF16), wp_ref[...])
    o_ref[...] = x + _rms(e, g_ref[...])


def _ple(x, p_all, layer, w_gate, w_proj, g, tm=1024):
    n = x.shape[0]
    return pl.pallas_call(
        _ple_body,
        grid=(n // tm,),
        in_specs=[
            pl.BlockSpec((tm, D_MODEL), lambda i: (i, 0)),
            pl.BlockSpec((None, tm, PLE_DIM), lambda i: (layer, i, 0)),
            _layer_block(layer, D_MODEL, D_MODEL),
            _layer_block(layer, PLE_DIM, D_MODEL),
            _layer_block(layer, 1, D_MODEL),
        ],
        out_specs=pl.BlockSpec((tm, D_MODEL), lambda i: (i, 0)),
        out_shape=jax.ShapeDtypeStruct((n, D_MODEL), F32),
        compiler_params=pltpu.CompilerParams(
            dimension_semantics=("parallel",), vmem_limit_bytes=VMEM_LIMIT),
        name="ple",
    )(x, p_all, w_gate, w_proj, g)


def _pad_lanes(v, offset):
    depth, heads = v.shape
    return jnp.zeros((depth, 1, LANES), F32).at[:, 0, offset:offset + heads].set(v.astype(F32))


def _expansion(offset, heads, width):
    r = jnp.arange(LANES)[:, None]
    c = jnp.arange(heads * width)[None, :]
    return (r == offset + c // width).astype(BF16)


def _split_w_in(w):
    i0 = 0
    z = w[..., i0:i0 + 1024]; i0 += 1024
    xs = w[..., i0:i0 + 1024]; i0 += 1024
    bc = w[..., i0:i0 + 512]; i0 += 512
    dt = w[..., i0:i0 + 16]; i0 += 16
    qkv = w[..., i0:i0 + 3072]; i0 += 3072
    gate = w[..., i0:i0 + 1024]; i0 += 1024
    b = w[..., i0:i0 + 8]; i0 += 8
    a = w[..., i0:i0 + 8]; i0 += 8
    mg = w[..., i0:i0 + 2048]
    big = jnp.concatenate([qkv, xs, z, gate, mg, bc], axis=-1).astype(BF16)
    small = jnp.concatenate([dt, b, a, jnp.zeros(w.shape[:-1] + (LANES - 32,), w.dtype)], axis=-1).astype(BF16)
    return big, small


PROJ_DTYPE = F32


def kernel(x, p, g_mix_pre, w_in, conv_ssm_w, conv_ssm_b, ssm_dt_bias, ssm_a_log, ssm_d, ssm_norm_g,
           conv_gdn_w, gdn_dt_bias, gdn_a_log, gdn_norm_g, w_br_ssm, w_br_gdn, w_out, g_mix_post,
           g_ffn_pre, w_ffn_gate, w_ffn_up, conv_ffn_w, conv_ffn_b, w_ffn_down, g_ffn_post,
           w_ple_gate, w_ple_proj, g_ple_post):
    batch, seq, d = x.shape
    depth = w_in.shape[0]
    n = batch * seq
    assert d == D_MODEL and seq % 1024 == 0
    xf = x.reshape(n, d)
    p_all = p.reshape(depth, n, PLE_DIM)

    rows = lambda v: v.astype(F32).reshape(depth, 1, -1)
    bf16 = lambda w: w.astype(BF16)
    w_big, w_small = _split_w_in(w_in)
    g_mix_pre, g_mix_post, g_ffn_pre, g_ffn_post, g_ple_post = map(
        rows, (g_mix_pre, g_mix_post, g_ffn_pre, g_ffn_post, g_ple_post))
    conv_ssm_w, conv_gdn_w, conv_ffn_w = (w.astype(F32) for w in (conv_ssm_w, conv_gdn_w, conv_ffn_w))
    conv_ssm_b, conv_ffn_b = rows(conv_ssm_b), rows(conv_ffn_b)
    ssm_dtb, ssm_alog = _pad_lanes(ssm_dt_bias, SM_DT), _pad_lanes(ssm_a_log, SM_DT)
    ssm_skip, ssm_norm_g = rows(jnp.repeat(ssm_d, SSM_HEAD_DIM, axis=1)), rows(ssm_norm_g)
    gdn_dtb, gdn_alog, gdn_norm_g = _pad_lanes(gdn_dt_bias, SM_A), _pad_lanes(gdn_a_log, SM_A), rows(gdn_norm_g)
    w_br_ssm, w_br_gdn, w_out, w_ffn_gate, w_ffn_up, w_ffn_down, w_ple_gate, w_ple_proj = map(
        bf16, (w_br_ssm, w_br_gdn, w_out, w_ffn_gate, w_ffn_up, w_ffn_down, w_ple_gate, w_ple_proj))
    e_ssd = _expansion(SM_DT, SSM_HEADS, SSM_HEAD_DIM)

    for i in range(depth):
        proj, small = _inproj(xf, g_mix_pre, w_big, w_small, i, PROJ_DTYPE)
        y_ssm = _ssd(proj, small, conv_ssm_w, conv_ssm_b, ssm_dtb, ssm_alog, ssm_skip, ssm_norm_g, e_ssd,
                     i, batch, seq)
        y_gdn = _gdn(proj, small, conv_gdn_w, gdn_dtb, gdn_alog, gdn_norm_g, i, batch, seq)
        xf = _mix(xf, y_ssm, y_gdn, proj, w_br_ssm, w_br_gdn, w_out, g_mix_post, i)
        xf = _ffn(xf, g_ffn_pre, w_ffn_gate, w_ffn_up, w_ffn_down, conv_ffn_w, conv_ffn_b, g_ffn_post, i, seq)
        xf = _ple(xf, p_all, i, w_ple_gate, w_ple_proj, g_ple_post)
    return xf.reshape(batch, seq, d)
```

```python
import functools

import jax
import jax.numpy as jnp
from jax import lax
from jax.experimental import pallas as pl
from jax.experimental.pallas import tpu as pltpu

F32 = jnp.float32
BF16 = jnp.bfloat16
EPS = 1e-6

D_MODEL = 1024
CHUNK = 64
SSM_HEADS = 16
SSM_HEAD_DIM = 64
SSM_D_INNER = 1024
SSM_GROUPS = 2
SSM_STATE = 128
SSM_CONV = 4
GDN_HEADS = 8
GDN_DK = 128
GDN_DV = 128
GDN_CONV = 4
D_FF = 4096
FFN_CONV = 3
PLE_DIM = 256

LANES = 128
SUBLANES = 8
MXU_COLS = 256
CONV_STRIDE = 4
FFN_HALO = 16
VMEM_LIMIT = 48 * 1024 * 1024

COL_Q, COL_K, COL_V, COL_XS, COL_Z, COL_GATE, COL_MGS, COL_MGG, COL_BC = (
    0, 1024, 2048, 3072, 4096, 5120, 6144, 7168, 8192)
PROJ_COLS = 8704
SM_DT, SM_B, SM_A = 0, 16, 24


def _rms(x, g):
    return x * lax.rsqrt(jnp.mean(x * x, axis=-1, keepdims=True) + EPS) * g


def _silu(x):
    return x * jax.nn.sigmoid(x)


def _softplus(x):
    return jnp.maximum(x, 0.0) + jnp.log1p(jnp.exp(-jnp.abs(x)))


def _gelu_tanh(x):
    c = 0.7978845608028654
    return 0.5 * x * (1.0 + jnp.tanh(c * (x + 0.044715 * (x * x * x))))


def _split2(v):
    hi = v.astype(BF16)
    lo = (v - hi.astype(F32)).astype(BF16)
    return hi, lo


def _split3(v):
    hi = v.astype(BF16)
    r = v - hi.astype(F32)
    mid = r.astype(BF16)
    lo = (r - mid.astype(F32)).astype(BF16)
    return hi, mid, lo


def _dot(a, b):
    return jnp.dot(a, b, preferred_element_type=F32)


def _dot_tb(a, b):
    return lax.dot_general(a, b, (((1,), (1,)), ((), ())), preferred_element_type=F32)


def _dot_ta(a, b):
    return lax.dot_general(a, b, (((0,), (0,)), ((), ())), preferred_element_type=F32)


def _expand(v, e):
    hi, lo = _split2(v)
    return _dot(hi, e) + _dot(lo, e)


def _conv_silu_tiles(x_ref, tail_ref, w_ref, b_ref, out_ref, r0, rows, taps):
    st = CONV_STRIDE
    assert taps - 1 <= st <= SUBLANES and rows % (SUBLANES * st) == 0 and r0 % (SUBLANES * st) == 0
    first_sublane = lax.broadcasted_iota(jnp.int32, (SUBLANES, LANES), 0) == 0
    for lt in range(x_ref.shape[0]):
        ln = slice(lt * LANES, (lt + 1) * LANES)
        w = [jnp.broadcast_to(w_ref[k:k + 1, ln], (SUBLANES, LANES)) for k in range(taps)]
        bias = None if b_ref is None else jnp.broadcast_to(b_ref[:, ln], (SUBLANES, LANES))
        for a in range(r0, r0 + rows, SUBLANES * st):
            v = [x_ref[lt, pl.ds(a + j, SUBLANES, stride=st), :] for j in range(st)]
            hist = []
            for d in range(1, taps):
                before = tail_ref[lt, SUBLANES - d:SUBLANES - d + 1, :] if a == 0 else x_ref[lt, a - d:a - d + 1, :]
                hist.append(jnp.where(first_sublane, before, pltpu.roll(v[st - d], 1, 0)))
            for j in range(st):
                acc = w[taps - 1] * v[j]
                for d in range(1, taps):
                    acc = acc + w[taps - 1 - d] * (v[j - d] if j >= d else hist[d - j - 1])
                if bias is not None:
                    acc = acc + bias
                out_ref[lt, pl.ds(a + j, SUBLANES, stride=st), :] = _silu(acc)


def _save_tail(x_ref, tail_ref):
    rows = x_ref.shape[1]
    for lt in range(x_ref.shape[0]):
        tail_ref[lt] = x_ref[lt, rows - SUBLANES:rows, :]


def _lanes(ref, r0, rows):
    return jnp.concatenate([ref[lt, r0:r0 + rows, :] for lt in range(ref.shape[0])], axis=1)


def _spread(main, extra):
    out = list(main)
    for k, item in enumerate(extra):
        out.insert((k + 1) * len(main) // (len(extra) + 1) + k, item)
    return out


def _emit(groups, side):
    done = 0
    for gi, grp in enumerate(groups):
        grp()
        upto = (gi + 1) * len(side) // len(groups)
        for item in side[done:upto]:
            item()
        done = upto


def _inproj_body(x_ref, g_ref, w_ref, ws_ref, o_ref, os_ref, h_ref):
    @pl.when(pl.program_id(1) == 0)
    def _():
        hb = _rms(x_ref[...], g_ref[...]).astype(BF16)
        h_ref[...] = hb
        os_ref[...] = _dot(hb, ws_ref[...])

    res = _dot(h_ref[...], w_ref[...]).astype(o_ref.dtype)
    for lt in range(o_ref.shape[0]):
        o_ref[lt] = res[:, lt * LANES:(lt + 1) * LANES]


def _layer_block(layer, *tail, col=0):
    index = (layer,) + (0,) * (len(tail) - 1) + (col,)
    return pl.BlockSpec((None,) + tail, lambda *_: index)


def _inproj(x, g, w_big, w_small, layer, proj_dtype, tm=1024, tn=2176):
    n = x.shape[0]
    return pl.pallas_call(
        _inproj_body,
        grid=(n // tm, PROJ_COLS // tn),
        in_specs=[
            pl.BlockSpec((tm, D_MODEL), lambda i, j: (i, 0)),
            _layer_block(layer, 1, D_MODEL),
            pl.BlockSpec((None, D_MODEL, tn), lambda i, j: (layer, 0, j)),
            _layer_block(layer, D_MODEL, LANES),
        ],
        out_specs=[
            pl.BlockSpec((tn // LANES, tm, LANES), lambda i, j: (j, i, 0)),
            pl.BlockSpec((tm, LANES), lambda i, j: (i, 0)),
        ],
        out_shape=[
            jax.ShapeDtypeStruct((PROJ_COLS // LANES, n, LANES), proj_dtype),
            jax.ShapeDtypeStruct((n, LANES), F32),
        ],
        scratch_shapes=[pltpu.VMEM((tm, D_MODEL), BF16)],
        compiler_params=pltpu.CompilerParams(
            dimension_semantics=("parallel", "arbitrary"), vmem_limit_bytes=VMEM_LIMIT),
        name="inproj",
    )(x, g, w_big, w_small)


def _ssd_body(xs_ref, bc_ref, z_ref, sm_ref, cwx_ref, cwbc_ref, cbx_ref, cbbc_ref, dtb_ref, alog_ref,
              dsk_ref, ng_ref, e_ref, o_ref, xtail, bctail, xc, bcc, state, *, rows_blk):
    q = CHUNK
    gw = SSM_D_INNER // SSM_GROUPS

    @pl.when(pl.program_id(1) == 0)
    def _():
        xtail[...] = jnp.zeros(xtail.shape, F32)
        bctail[...] = jnp.zeros(bctail.shape, F32)
        state[...] = jnp.zeros(state.shape, F32)

    _conv_silu_tiles(xs_ref, xtail, cwx_ref, cbx_ref, xc, 0, rows_blk, SSM_CONV)
    _conv_silu_tiles(bc_ref, bctail, cwbc_ref, cbbc_ref, bcc, 0, rows_blk, SSM_CONV)
    _save_tail(xs_ref, xtail)
    _save_tail(bc_ref, bctail)

    lane = lax.broadcasted_iota(jnp.int32, (q, LANES), 1)
    rowi = lax.broadcasted_iota(jnp.int32, (q, LANES), 0)
    lo_half = lane < q
    causal2 = rowi >= (lane % q)
    li = lax.broadcasted_iota(jnp.int32, (q, q), 0)
    lj = lax.broadcasted_iota(jnp.int32, (q, q), 1)
    l_incl = (lj <= li).astype(BF16)
    u2 = (rowi <= (lane % q)).astype(BF16)
    r2 = lax.broadcasted_iota(jnp.int32, (2 * q, LANES), 0)
    c2 = lax.broadcasted_iota(jnp.int32, (2 * q, LANES), 1)
    blockdiag = (r2 < q) == (c2 < q)
    dt_lane = lax.broadcasted_iota(jnp.int32, (q, LANES), 1) < SSM_HEADS

    a_vec = -jnp.exp(alog_ref[...])
    e_mat = e_ref[...]

    nc = rows_blk // q
    live = [dict() for _ in range(nc)]

    def b_of(c, g):
        return bcc[g, c * q:(c + 1) * q, :].astype(BF16)

    def c_of(c, g):
        return bcc[SSM_GROUPS + g, c * q:(c + 1) * q, :].astype(BF16)

    def cumulative(c):
        v = live[c]
        v["dt"] = jnp.where(dt_lane, _softplus(sm_ref[c * q:(c + 1) * q, :] + dtb_ref[...]), 0.0)
        parts = _split3(v["dt"] * a_vec)
        v["acum"] = _dot(l_incl, parts[0]) + _dot(l_incl, parts[1]) + _dot(l_incl, parts[2])
        v["acum_t"] = _dot_ta(parts[0], u2) + _dot_ta(parts[1], u2) + _dot_ta(parts[2], u2)

    def expand(c):
        v = live[c]
        ea = jnp.exp(v["acum"])
        ds = jnp.exp(v["acum"][q - 1:q, :] - v["acum"])
        ex = _expand(jnp.concatenate([v.pop("dt"), ea, ds], axis=0), e_mat)
        v["ea_e"] = ex[q:2 * q]
        v["xdt"] = _lanes(xc, c * q, q) * ex[0:q]
        v["xds"] = (v["xdt"] * ex[2 * q:3 * q]).astype(BF16)

    def in_chunk(c):
        v = live[c]
        xds = v.pop("xds")
        v["contrib"] = [_dot_ta(b_of(c, g), xds[:, g * gw:(g + 1) * gw]) for g in range(SSM_GROUPS)]
        v["y"] = []
        for g in range(SSM_GROUPS):
            bg = b_of(c, g)
            sc2 = _dot_tb(c_of(c, g), jnp.concatenate([bg, bg], axis=0))
            for pr in range(gw // LANES):
                h0 = g * (SSM_HEADS // SSM_GROUPS) + 2 * pr
                l0 = g * gw + pr * LANES
                col = jnp.where(lo_half, v["acum"][:, h0:h0 + 1], v["acum"][:, h0 + 1:h0 + 2])
                row = jnp.where(lo_half, v["acum_t"][h0:h0 + 1, :], v["acum_t"][h0 + 1:h0 + 2, :])
                lm = jnp.exp(jnp.where(causal2, col - row, -jnp.inf))
                pm = (sc2 * lm).astype(BF16)
                xp = v["xdt"][:, l0:l0 + LANES].astype(BF16)
                rhs = jnp.where(blockdiag, jnp.concatenate([xp, xp], axis=0), jnp.zeros((), BF16))
                v["y"].append(_dot(pm, rhs))
        del v["acum"], v["acum_t"], v["xdt"]

    def carry(c):
        v = live[c]
        y_parts = []
        for g in range(SSM_GROUPS):
            gl = slice(g * gw, (g + 1) * gw)
            st = state[:, gl]
            yoff = _dot(c_of(c, g), st.astype(BF16)) * v["ea_e"][:, gl]
            state[:, gl] = st * v["ea_e"][q - 1:q, gl] + v["contrib"][g]
            for pr in range(gw // LANES):
                y_parts.append(v["y"][g * (gw // LANES) + pr] + yoff[:, pr * LANES:(pr + 1) * LANES])
        y = jnp.concatenate(y_parts, axis=1) + dsk_ref[...] * _lanes(xc, c * q, q)
        yz = y * _silu(_lanes(z_ref, c * q, q).astype(F32))
        outs = []
        for g in range(SSM_GROUPS):
            part = yz[:, g * gw:(g + 1) * gw]
            outs.append(part * lax.rsqrt(jnp.mean(part * part, axis=-1, keepdims=True) + EPS))
        o_ref[c * q:(c + 1) * q, :] = (jnp.concatenate(outs, axis=1) * ng_ref[...]).astype(o_ref.dtype)
        live[c].clear()

    stages = (cumulative, expand, in_chunk, carry)
    for t in range(nc + len(stages) - 1):
        for k, stage in enumerate(stages):
            if 0 <= t - k < nc:
                stage(t - k)


def _ssd(proj, small, conv_w, conv_b, dtb, alog, dsk, ng, e_mat, layer, batch, seq, rows_blk=1024):
    nt = seq // rows_blk
    n = batch * seq
    bcw = 2 * SSM_GROUPS * SSM_STATE
    xt = SSM_D_INNER // LANES
    bt = bcw // LANES
    row = lambda b, t: b * nt + t
    full = lambda shape: pl.BlockSpec(shape, lambda b, t: (0,) * len(shape))
    return pl.pallas_call(
        functools.partial(_ssd_body, rows_blk=rows_blk),
        grid=(batch, nt),
        in_specs=[
            pl.BlockSpec((xt, rows_blk, LANES), lambda b, t: (COL_XS // SSM_D_INNER, row(b, t), 0)),
            pl.BlockSpec((bt, rows_blk, LANES), lambda b, t: (COL_BC // bcw, row(b, t), 0)),
            pl.BlockSpec((xt, rows_blk, LANES), lambda b, t: (COL_Z // SSM_D_INNER, row(b, t), 0)),
            pl.BlockSpec((rows_blk, LANES), lambda b, t: (row(b, t), 0)),
            _layer_block(layer, SSM_CONV, SSM_D_INNER), _layer_block(layer, SSM_CONV, bcw, col=SSM_D_INNER // bcw),
            _layer_block(layer, 1, SSM_D_INNER), _layer_block(layer, 1, bcw, col=SSM_D_INNER // bcw),
            _layer_block(layer, 1, LANES), _layer_block(layer, 1, LANES),
            _layer_block(layer, 1, SSM_D_INNER), _layer_block(layer, 1, SSM_D_INNER),
            full((LANES, SSM_D_INNER)),
        ],
        out_specs=pl.BlockSpec((rows_blk, SSM_D_INNER), lambda b, t: (row(b, t), 0)),
        out_shape=jax.ShapeDtypeStruct((n, SSM_D_INNER), BF16),
        scratch_shapes=[
            pltpu.VMEM((xt, SUBLANES, LANES), F32),
            pltpu.VMEM((bt, SUBLANES, LANES), F32),
            pltpu.VMEM((xt, rows_blk, LANES), F32),
            pltpu.VMEM((bt, rows_blk, LANES), F32),
            pltpu.VMEM((SSM_STATE, SSM_D_INNER), F32),
        ],
        compiler_params=pltpu.CompilerParams(
            dimension_semantics=("parallel", "arbitrary"), vmem_limit_bytes=VMEM_LIMIT),
        name="ssd",
    )(proj, proj, proj, small, conv_w, conv_w, conv_b, conv_b, dtb, alog, dsk, ng, e_mat)


def _gdn_body(q_ref, k_ref, v_ref, gate_ref, sm_ref, cwq_ref, cwk_ref, cwv_ref, dtb_ref, alog_ref, ng_ref,
              o_ref,
              qtail, ktail, vtail, qc, kc, vc, gc_ref, gct_ref, beta_ref, eg_ref, ekd_ref,
              kbq_ref, kn_ref, dec_ref, m_ref, p_ref, qk_ref, rhs_ref, wq_ref, u_ref, kd_ref, vn_ref, os_ref,
              last_ref, s_ref,
              *, rows_blk):
    sup = 2 * CHUNK
    nsc = rows_blk // sup
    nh = GDN_HEADS

    @pl.when(pl.program_id(1) == 0)
    def _():
        qtail[...] = jnp.zeros(qtail.shape, F32)
        ktail[...] = jnp.zeros(ktail.shape, F32)
        vtail[...] = jnp.zeros(vtail.shape, F32)
        s_ref[...] = jnp.zeros(s_ref.shape, F32)

    def conv(sc, stream):
        src, tail, cw, dst = ((q_ref, qtail, cwq_ref, qc), (k_ref, ktail, cwk_ref, kc),
                              (v_ref, vtail, cwv_ref, vc))[stream]
        _conv_silu_tiles(src, tail, cw, None, dst, sc * sup, sup, GDN_CONV)

    ri = lax.broadcasted_iota(jnp.int32, (sup, sup), 0)
    ci = lax.broadcasted_iota(jnp.int32, (sup, sup), 1)
    same = (ri < CHUNK) == (ci < CHUNK)
    incl = same & (ci <= ri)
    strict = same & (ci < ri)
    l_bd = incl.astype(BF16)
    u_bd = (same & (ri <= ci)).astype(BF16)
    eye = (ri == ci).astype(F32)
    lane = lax.broadcasted_iota(jnp.int32, (sup, LANES), 1)
    g_lane = (lane >= SM_A) & (lane < SM_A + GDN_HEADS)
    first_chunk = lax.broadcasted_iota(jnp.int32, (sup, LANES), 0) < CHUNK

    a_vec = -jnp.exp(alog_ref[...])
    scale = GDN_DK ** -0.5

    def pair(sc, h):
        return sc * nh + h

    def prep(sc):
        r0 = sc * sup
        sm = sm_ref[r0:r0 + sup, :]
        gl = jnp.where(g_lane, a_vec * _softplus(sm + dtb_ref[...]), 0.0)
        parts = _split3(gl)
        gc = _dot(l_bd, parts[0]) + _dot(l_bd, parts[1]) + _dot(l_bd, parts[2])
        gc_ref[...] = gc
        gct_ref[...] = _dot_ta(parts[0], u_bd) + _dot_ta(parts[1], u_bd) + _dot_ta(parts[2], u_bd)
        eg = jnp.exp(gc)
        gc_last = jnp.where(first_chunk, gc[CHUNK - 1:CHUNK, :], gc[sup - 1:sup, :])
        beta_ref[...] = jax.nn.sigmoid(sm)
        eg_ref[...] = eg
        ekd_ref[...] = jnp.exp(gc_last - gc)
        for c in range(2):
            last_ref[2 * sc + c] = jnp.broadcast_to(eg[(c + 1) * CHUNK - 1:(c + 1) * CHUNK, :], (SUBLANES, LANES))

    def a_vector(sc, h):
        p = pair(sc, h)
        r0 = sc * sup
        ln = slice(h * GDN_DK, (h + 1) * GDN_DK)
        qh = qc[h, r0:r0 + sup, :]
        kh = kc[h, r0:r0 + sup, :]
        vh = vc[h, r0:r0 + sup, :]
        qs = qh * (lax.rsqrt(jnp.sum(qh * qh, axis=-1, keepdims=True) + EPS) * scale)
        kn = kh * lax.rsqrt(jnp.sum(kh * kh, axis=-1, keepdims=True) + EPS)
        b_h = jnp.broadcast_to(beta_ref[:, SM_B + h:SM_B + h + 1], (sup, LANES))
        eg_h = jnp.broadcast_to(eg_ref[:, SM_A + h:SM_A + h + 1], (sup, LANES))
        ekd_h = jnp.broadcast_to(ekd_ref[:, SM_A + h:SM_A + h + 1], (sup, LANES))
        kb = kn * b_h
        kbq_ref[p] = jnp.concatenate([kb, qs], axis=0).astype(BF16)
        kn_ref[p] = kn.astype(BF16)
        col = gc_ref[:, SM_A + h:SM_A + h + 1]
        row = gct_ref[SM_A + h:SM_A + h + 1, :]
        dec_ref[p] = jnp.exp(jnp.where(incl, col - row, -jnp.inf))
        rhs_ref[p] = jnp.concatenate([vh * b_h, kb * eg_h], axis=1).astype(BF16)
        qd = (qs * eg_h).astype(BF16)
        for c in range(2):
            wq_ref[2 * p + c, CHUNK:2 * CHUNK, :] = qd[c * CHUNK:(c + 1) * CHUNK]
        kd_ref[p] = (kn * ekd_h).astype(BF16)

    def a_matmul(sc):
        for h in range(nh):
            p = pair(sc, h)
            kq = _dot_tb(kbq_ref[p], kn_ref[p])
            dec = dec_ref[p]
            nmat = jnp.where(strict, -(kq[0:sup] * dec), 0.0)
            m_ref[p] = nmat.astype(BF16)
            p_ref[p] = eye + nmat
            qk_ref[p] = (kq[sup:2 * sup] * dec).astype(BF16)

    def square(p):
        mb = m_ref[p]
        m_ref[p] = _dot(mb, mb).astype(BF16)

    def accumulate(p):
        pv = p_ref[p]
        p_ref[p] = pv + _dot(pv.astype(BF16), m_ref[p])

    def phase_b_groups(sc):
        groups = []
        for _ in range(5):
            groups.append(lambda: [square(pair(sc, h)) for h in range(nh)])
            groups.append(lambda: [accumulate(pair(sc, h)) for h in range(nh)])
        return groups

    def phase_c(sc):
        for h in range(nh):
            p = pair(sc, h)
            uw = _dot(p_ref[p].astype(BF16), rhs_ref[p])
            u_ref[p] = uw[:, 0:GDN_DV]
            w = uw[:, GDN_DV:2 * GDN_DV].astype(BF16)
            for c in range(2):
                wq_ref[2 * p + c, 0:CHUNK, :] = w[c * CHUNK:(c + 1) * CHUNK]

    def d_apply(sc, c):
        rows = slice(c * CHUNK, (c + 1) * CHUNK)
        for h in range(nh):
            p = pair(sc, h)
            r = _dot(wq_ref[2 * p + c], s_ref[h].astype(BF16))
            vn_ref[p, rows, :] = (u_ref[p, rows, :] - r[0:CHUNK]).astype(BF16)
            os_ref[p, rows, :] = r[CHUNK:2 * CHUNK]

    def d_update(sc, c):
        rows = slice(c * CHUNK, (c + 1) * CHUNK)
        for h in range(nh):
            p = pair(sc, h)
            last = last_ref[2 * sc + c, 0:1, SM_A + h:SM_A + h + 1]
            s_ref[h] = s_ref[h] * last + _dot_ta(kd_ref[p, rows, :], vn_ref[p, rows, :])

    def phase_d_items(sc):
        items = []
        for c in range(2):
            items.append(functools.partial(d_apply, sc, c))
            items.append(functools.partial(d_update, sc, c))
        return items

    def phase_e(sc, h):
        p = pair(sc, h)
        r0 = sc * sup
        ln = slice(h * GDN_DK, (h + 1) * GDN_DK)
        o = os_ref[p] + _dot(qk_ref[p], vn_ref[p])
        o = o * lax.rsqrt(jnp.mean(o * o, axis=-1, keepdims=True) + EPS) * ng_ref[...]
        o = o * _silu(gate_ref[h, r0:r0 + sup, :].astype(F32))
        o_ref[r0:r0 + sup, ln] = o.astype(o_ref.dtype)

    def phase_e_items(sc):
        return [functools.partial(phase_e, sc, h) for h in range(nh)]

    def a_vector_items(sc):
        return ([functools.partial(prep, sc)] + [functools.partial(conv, sc, s) for s in range(3)]
                + [functools.partial(a_vector, sc, h) for h in range(nh)])

    for item in a_vector_items(0):
        item()
    a_matmul(0)
    for sc in range(nsc):
        side = _spread(a_vector_items(sc + 1) if sc + 1 < nsc else [],
                       phase_d_items(sc - 1) if sc >= 1 else [])
        _emit(phase_b_groups(sc), side)
        phase_c(sc)
        if sc + 1 < nsc:
            a_matmul(sc + 1)
        if sc >= 1:
            for item in phase_e_items(sc - 1):
                item()
    for item in phase_d_items(nsc - 1) + phase_e_items(nsc - 1):
        item()
    _save_tail(q_ref, qtail)
    _save_tail(k_ref, ktail)
    _save_tail(v_ref, vtail)


def _gdn(proj, small, conv_w, dtb, alog, ng, layer, batch, seq, rows_blk=512):
    nt = seq // rows_blk
    n = batch * seq
    width = GDN_HEADS * GDN_DK
    sup = 2 * CHUNK
    npair = (rows_blk // sup) * GDN_HEADS
    row = lambda b, t: b * nt + t
    colblk = lambda col: pl.BlockSpec((GDN_HEADS, rows_blk, LANES), lambda b, t: (col // width, row(b, t), 0))
    return pl.pallas_call(
        functools.partial(_gdn_body, rows_blk=rows_blk),
        grid=(batch, nt),
        in_specs=[
            colblk(COL_Q), colblk(COL_K), colblk(COL_V), colblk(COL_GATE),
            pl.BlockSpec((rows_blk, LANES), lambda b, t: (row(b, t), 0)),
            _layer_block(layer, GDN_CONV, width, col=0), _layer_block(layer, GDN_CONV, width, col=1),
            _layer_block(layer, GDN_CONV, width, col=2),
            _layer_block(layer, 1, LANES), _layer_block(layer, 1, LANES), _layer_block(layer, 1, GDN_DV),
        ],
        out_specs=pl.BlockSpec((rows_blk, width), lambda b, t: (row(b, t), 0)),
        out_shape=jax.ShapeDtypeStruct((n, width), BF16),
        scratch_shapes=[
            pltpu.VMEM((GDN_HEADS, SUBLANES, LANES), F32),
            pltpu.VMEM((GDN_HEADS, SUBLANES, LANES), F32),
            pltpu.VMEM((GDN_HEADS, SUBLANES, LANES), F32),
            pltpu.VMEM((GDN_HEADS, rows_blk, LANES), F32),
            pltpu.VMEM((GDN_HEADS, rows_blk, LANES), F32),
            pltpu.VMEM((GDN_HEADS, rows_blk, LANES), F32),
            pltpu.VMEM((sup, LANES), F32),
            pltpu.VMEM((LANES, sup), F32),
            pltpu.VMEM((sup, LANES), F32),
            pltpu.VMEM((sup, LANES), F32),
            pltpu.VMEM((sup, LANES), F32),
            pltpu.VMEM((npair, 2 * sup, GDN_DK), BF16),
            pltpu.VMEM((npair, sup, GDN_DK), BF16),
            pltpu.VMEM((npair, sup, sup), F32),
            pltpu.VMEM((npair, sup, sup), BF16),
            pltpu.VMEM((npair, sup, sup), F32),
            pltpu.VMEM((npair, sup, sup), BF16),
            pltpu.VMEM((npair, sup, 2 * GDN_DV), BF16),
            pltpu.VMEM((2 * npair, sup, GDN_DK), BF16),
            pltpu.VMEM((npair, sup, GDN_DV), F32),
            pltpu.VMEM((npair, sup, GDN_DK), BF16),
            pltpu.VMEM((npair, sup, GDN_DV), BF16),
            pltpu.VMEM((npair, sup, GDN_DV), F32),
            pltpu.VMEM((2 * (rows_blk // sup), SUBLANES, LANES), F32),
            pltpu.VMEM((GDN_HEADS, GDN_DK, GDN_DV), F32),
        ],
        compiler_params=pltpu.CompilerParams(
            dimension_semantics=("parallel", "arbitrary"), vmem_limit_bytes=VMEM_LIMIT),
        name="gdn",
    )(proj, proj, proj, proj, small, conv_w, conv_w, conv_w, dtb, alog, ng)


def _mix_body(x_ref, ys_ref, yg_ref, ms_ref, mg_ref, ws_ref, wg_ref, wo_ref, g_ref, o_ref):
    a = _dot(ys_ref[...], ws_ref[...])
    b = _dot(yg_ref[...], wg_ref[...])
    rows = x_ref.shape[0]
    ms = _lanes(ms_ref, 0, rows).astype(F32)
    mg = _lanes(mg_ref, 0, rows).astype(F32)
    mixed = jax.nn.sigmoid(ms) * a + jax.nn.sigmoid(mg) * b
    out = _dot(mixed.astype(BF16), wo_ref[...])
    o_ref[...] = x_ref[...] + _rms(out, g_ref[...])


def _mix(x, y_ssm, y_gdn, proj, w_s, w_g, w_o, g, layer, tm=512):
    n = x.shape[0]
    rowblk = lambda col: pl.BlockSpec((tm, D_MODEL), lambda i: (i, col // D_MODEL))
    slabs = lambda col: pl.BlockSpec((D_MODEL // LANES, tm, LANES), lambda i: (col // D_MODEL, i, 0))
    weight = _layer_block(layer, D_MODEL, D_MODEL)
    return pl.pallas_call(
        _mix_body,
        grid=(n // tm,),
        in_specs=[rowblk(0), rowblk(0), rowblk(0), slabs(COL_MGS), slabs(COL_MGG),
                  weight, weight, weight, _layer_block(layer, 1, D_MODEL)],
        out_specs=rowblk(0),
        out_shape=jax.ShapeDtypeStruct((n, D_MODEL), F32),
        compiler_params=pltpu.CompilerParams(
            dimension_semantics=("parallel",), vmem_limit_bytes=VMEM_LIMIT),
        name="mix",
    )(x, y_ssm, y_gdn, proj, proj, w_s, w_g, w_o, g)


def _ffn_body(x_ref, xp_ref, gpre_ref, wg_ref, wu_ref, wd_ref, cw_ref, cb_ref, gpost_ref, o_ref,
              h_ref, gbuf, acc_ref, *, tm, tf, tiles_per_seq):
    i = pl.program_id(0)
    j = pl.program_id(1)

    @pl.when(j == 0)
    def _():
        h_ref[FFN_HALO:FFN_HALO + tm, :] = _rms(x_ref[...], gpre_ref[...]).astype(BF16)
        h_ref[0:FFN_HALO, :] = _rms(xp_ref[...], gpre_ref[...]).astype(BF16)
        acc_ref[...] = jnp.zeros(acc_ref.shape, F32)

    seq_start = i % tiles_per_seq == 0

    def gate_up(s):
        cs = slice(s * MXU_COLS, (s + 1) * MXU_COLS)
        g = _dot(h_ref[...], wg_ref[:, cs])
        gbuf[FFN_HALO:FFN_HALO + tm, cs] = g[FFN_HALO:FFN_HALO + tm]
        gbuf[0:FFN_HALO, cs] = jnp.where(seq_start, 0.0, g[0:FFN_HALO])
        return _dot(h_ref[FFN_HALO:FFN_HALO + tm, :], wu_ref[:, cs])

    def act_down(s, up):
        cs = slice(s * MXU_COLS, (s + 1) * MXU_COLS)
        conv = cb_ref[:, cs]
        for k in range(FFN_CONV):
            off = FFN_HALO - (FFN_CONV - 1) + k
            conv = conv + cw_ref[k:k + 1, cs] * gbuf[off:off + tm, cs]
        acc_ref[...] += _dot((_gelu_tanh(conv) * up).astype(BF16), wd_ref[cs, :])

    ns = tf // MXU_COLS
    up = gate_up(0)
    for s in range(ns):
        up_next = gate_up(s + 1) if s + 1 < ns else None
        act_down(s, up)
        up = up_next

    @pl.when(j == pl.num_programs(1) - 1)
    def _():
        o_ref[...] = x_ref[...] + _rms(acc_ref[...], gpost_ref[...])


def _ffn(x, gpre, w_gate, w_up, w_down, cw, cb, gpost, layer, seq, tm=1024, tf=512):
    n = x.shape[0]
    hb = tm // FFN_HALO
    cols = lambda rows: pl.BlockSpec((None, rows, tf), lambda i, j: (layer, 0, j))
    return pl.pallas_call(
        functools.partial(_ffn_body, tm=tm, tf=tf, tiles_per_seq=seq // tm),
        grid=(n // tm, D_FF // tf),
        in_specs=[
            pl.BlockSpec((tm, D_MODEL), lambda i, j: (i, 0)),
            pl.BlockSpec((FFN_HALO, D_MODEL), lambda i, j: (jnp.maximum(i * hb - 1, 0), 0)),
            _layer_block(layer, 1, D_MODEL),
            cols(D_MODEL), cols(D_MODEL),
            pl.BlockSpec((None, tf, D_MODEL), lambda i, j: (layer, j, 0)),
            cols(FFN_CONV), cols(1),
            _layer_block(layer, 1, D_MODEL),
        ],
        out_specs=pl.BlockSpec((tm, D_MODEL), lambda i, j: (i, 0)),
        out_shape=jax.ShapeDtypeStruct((n, D_MODEL), F32),
        scratch_shapes=[
            pltpu.VMEM((FFN_HALO + tm, D_MODEL), BF16),
            pltpu.VMEM((FFN_HALO + tm, tf), F32),
            pltpu.VMEM((tm, D_MODEL), F32),
        ],
        compiler_params=pltpu.CompilerParams(
            dimension_semantics=("parallel", "arbitrary"), vmem_limit_bytes=VMEM_LIMIT),
        name="ffn",
    )(x, x, gpre, w_gate, w_up, w_down, cw, cb, gpost)


def _ple_body(x_ref, p_ref, wg_ref, wp_ref, g_ref, o_ref):
    x = x_ref[...]
    e = jax.nn.sigmoid(_dot(x.astype(BF16), wg_ref[...])) * _dot(p_ref[...].astype(BF16), wp_ref[...])
    o_ref[...] = x + _rms(e, g_ref[...])


def _ple(x, p_all, layer, w_gate, w_proj, g, tm=1024):
    n = x.shape[0]
    return pl.pallas_call(
        _ple_body,
        grid=(n // tm,),
        in_specs=[
            pl.BlockSpec((tm, D_MODEL), lambda i: (i, 0)),
            pl.BlockSpec((None, tm, PLE_DIM), lambda i: (layer, i, 0)),
            _layer_block(layer, D_MODEL, D_MODEL),
            _layer_block(layer, PLE_DIM, D_MODEL),
            _layer_block(layer, 1, D_MODEL),
        ],
        out_specs=pl.BlockSpec((tm, D_MODEL), lambda i: (i, 0)),
        out_shape=jax.ShapeDtypeStruct((n, D_MODEL), F32),
        compiler_params=pltpu.CompilerParams(
            dimension_semantics=("parallel",), vmem_limit_bytes=VMEM_LIMIT),
        name="ple",
    )(x, p_all, w_gate, w_proj, g)


def _pad_lanes(v, offset):
    depth, heads = v.shape
    return jnp.zeros((depth, 1, LANES), F32).at[:, 0, offset:offset + heads].set(v.astype(F32))


def _expansion(offset, heads, width):
    r = jnp.arange(LANES)[:, None]
    c = jnp.arange(heads * width)[None, :]
    return (r == offset + c // width).astype(BF16)


def _split_w_in(w):
    i0 = 0
    z = w[..., i0:i0 + 1024]; i0 += 1024
    xs = w[..., i0:i0 + 1024]; i0 += 1024
    bc = w[..., i0:i0 + 512]; i0 += 512
    dt = w[..., i0:i0 + 16]; i0 += 16
    qkv = w[..., i0:i0 + 3072]; i0 += 3072
    gate = w[..., i0:i0 + 1024]; i0 += 1024
    b = w[..., i0:i0 + 8]; i0 += 8
    a = w[..., i0:i0 + 8]; i0 += 8
    mg = w[..., i0:i0 + 2048]
    big = jnp.concatenate([qkv, xs, z, gate, mg, bc], axis=-1).astype(BF16)
    small = jnp.concatenate([dt, b, a, jnp.zeros(w.shape[:-1] + (LANES - 32,), w.dtype)], axis=-1).astype(BF16)
    return big, small


PROJ_DTYPE = F32


def kernel(x, p, g_mix_pre, w_in, conv_ssm_w, conv_ssm_b, ssm_dt_bias, ssm_a_log, ssm_d, ssm_norm_g,
           conv_gdn_w, gdn_dt_bias, gdn_a_log, gdn_norm_g, w_br_ssm, w_br_gdn, w_out, g_mix_post,
           g_ffn_pre, w_ffn_gate, w_ffn_up, conv_ffn_w, conv_ffn_b, w_ffn_down, g_ffn_post,
           w_ple_gate, w_ple_proj, g_ple_post):
    batch, seq, d = x.shape
    depth = w_in.shape[0]
    n = batch * seq
    assert d == D_MODEL and seq % 1024 == 0
    xf = x.reshape(n, d)
    p_all = p.reshape(depth, n, PLE_DIM)

    rows = lambda v: v.astype(F32).reshape(depth, 1, -1)
    bf16 = lambda w: w.astype(BF16)
    w_big, w_small = _split_w_in(w_in)
    g_mix_pre, g_mix_post, g_ffn_pre, g_ffn_post, g_ple_post = map(
        rows, (g_mix_pre, g_mix_post, g_ffn_pre, g_ffn_post, g_ple_post))
    conv_ssm_w, conv_gdn_w, conv_ffn_w = (w.astype(F32) for w in (conv_ssm_w, conv_gdn_w, conv_ffn_w))
    conv_ssm_b, conv_ffn_b = rows(conv_ssm_b), rows(conv_ffn_b)
    ssm_dtb, ssm_alog = _pad_lanes(ssm_dt_bias, SM_DT), _pad_lanes(ssm_a_log, SM_DT)
    ssm_skip, ssm_norm_g = rows(jnp.repeat(ssm_d, SSM_HEAD_DIM, axis=1)), rows(ssm_norm_g)
    gdn_dtb, gdn_alog, gdn_norm_g = _pad_lanes(gdn_dt_bias, SM_A), _pad_lanes(gdn_a_log, SM_A), rows(gdn_norm_g)
    w_br_ssm, w_br_gdn, w_out, w_ffn_gate, w_ffn_up, w_ffn_down, w_ple_gate, w_ple_proj = map(
        bf16, (w_br_ssm, w_br_gdn, w_out, w_ffn_gate, w_ffn_up, w_ffn_down, w_ple_gate, w_ple_proj))
    e_ssd = _expansion(SM_DT, SSM_HEADS, SSM_HEAD_DIM)

    for i in range(depth):
        proj, small = _inproj(xf, g_mix_pre, w_big, w_small, i, PROJ_DTYPE)
        y_ssm = _ssd(proj, small, conv_ssm_w, conv_ssm_b, ssm_dtb, ssm_alog, ssm_skip, ssm_norm_g, e_ssd,
                     i, batch, seq)
        y_gdn = _gdn(proj, small, conv_gdn_w, gdn_dtb, gdn_alog, gdn_norm_g, i, batch, seq)
        xf = _mix(xf, y_ssm, y_gdn, proj, w_br_ssm, w_br_gdn, w_out, g_mix_post, i)
        xf = _ffn(xf, g_ffn_pre, w_ffn_gate, w_ffn_up, w_ffn_down, conv_ffn_w, conv_ffn_b, g_ffn_post, i, seq)
        xf = _ple(xf, p_all, i, w_ple_gate, w_ple_proj, g_ple_post)
    return xf.reshape(batch, seq, d)
```
